```python
import math
import jax, jax.numpy as jnp
from jax import lax
import numpy as np

D_MODEL = 1024
BATCH = 32
SEQ = 2048
DEPTH = 1
DEC_BATCH = 128
DEC_SEQ = 1
PAST_LEN = 16384
PAGE_SIZE = 128

N_HEADS = 16
QK_NOPE = 64
QK_ROPE = 32
QK_DIM = QK_NOPE + QK_ROPE
V_HEAD = 64
Q_LORA = 384
KV_LORA = 256
ROPE_THETA = 10000.0
CONV_CH = 1024
CONV_WIDTH = 31
CONV_BUF = CONV_WIDTH - 1
D_FF = 2816
N_BRANCH = 2
N_MOD = 9
Q_BLOCK = 128
EPS = 1e-6
IN_COLS = Q_LORA + KV_LORA + QK_ROPE + 2 * CONV_CH + N_BRANCH * D_MODEL

kernel_name = 'hybrid_mla_conformer_conv_macaron_step'


def _rmsnorm(x, g):
    xf = x.astype(jnp.float32)
    y = xf * lax.rsqrt(jnp.mean(xf * xf, axis=-1, keepdims=True) + EPS)
    return (y * g.astype(jnp.float32)).astype(x.dtype)


def _layernorm(x, g, b):
    xf = x.astype(jnp.float32)
    mu = jnp.mean(xf, axis=-1, keepdims=True)
    var = jnp.mean(jnp.square(xf - mu), axis=-1, keepdims=True)
    y = (xf - mu) * lax.rsqrt(var + EPS)
    return (y * g.astype(jnp.float32) + b.astype(jnp.float32)).astype(x.dtype)


def _modulate(xn, shift, scale):
    return xn * (1 + scale) + shift


def _swiglu(x, w_in, w_out):
    up, gt = jnp.split(x @ w_in, 2, axis=-1)
    return (jax.nn.silu(gt) * up) @ w_out


def _adaln(c, w_ada, b_ada):
    mod = jax.nn.silu(c) @ w_ada + b_ada
    return jnp.split(mod[:, None, :], N_MOD, axis=-1)


def _rope(x, pos):
    half = x.shape[-1] // 2
    inv = ROPE_THETA ** (-jnp.arange(half, dtype=jnp.float32) / half)
    ang = pos.astype(jnp.float32)[:, None] * inv[None, :]
    cos = jnp.cos(ang)[:, None, :]
    sin = jnp.sin(ang)[:, None, :]
    xf = x.astype(jnp.float32)
    x1, x2 = xf[..., :half], xf[..., half:]
    out = jnp.concatenate([x1 * cos - x2 * sin, x1 * sin + x2 * cos], axis=-1)
    return out.astype(x.dtype)


def _split_in(p):
    cuts = [Q_LORA, Q_LORA + KV_LORA, Q_LORA + KV_LORA + QK_ROPE,
            Q_LORA + KV_LORA + QK_ROPE + 2 * CONV_CH]
    return jnp.split(p, cuts, axis=-1)


def _mla_query(q_a, pos, g_q_lat, w_uq, g_q_head):
    q = jnp.einsum('btr,rhd->bthd', _rmsnorm(q_a, g_q_lat), w_uq)
    q = jnp.concatenate([q[..., :QK_NOPE], _rope(q[..., QK_NOPE:], pos)], axis=-1)
    return _rmsnorm(q, g_q_head)


def _mla_latent(kv_a, k_pe, pos, g_kv_lat):
    return _rmsnorm(kv_a, g_kv_lat), _rope(k_pe[:, :, None, :], pos)[:, :, 0, :]


def _mla_keys(kv_lat, k_rope, w_uk, g_k_head):
    k_nope = jnp.einsum('btl,lhd->bthd', kv_lat, w_uk)
    k_r = jnp.broadcast_to(k_rope[:, :, None, :], k_nope.shape[:-1] + (QK_ROPE,)).astype(k_nope.dtype)
    return _rmsnorm(jnp.concatenate([k_nope, k_r], axis=-1), g_k_head)


def _attn_prompt(q, kv_lat, k_rope, w_uk, w_uv, g_k_head):
    b, t = q.shape[0], q.shape[1]
    k = _mla_keys(kv_lat, k_rope, w_uk, g_k_head)
    v = jnp.einsum('btl,lhd->bthd', kv_lat, w_uv).astype(jnp.float32)
    n_blk = t // Q_BLOCK
    q_blocks = q.reshape(b, n_blk, Q_BLOCK, N_HEADS, QK_DIM).swapaxes(0, 1)
    q_pos = jnp.arange(t, dtype=jnp.int32).reshape(n_blk, Q_BLOCK)
    k_pos = jnp.arange(t, dtype=jnp.int32)
    scale = QK_DIM ** -0.5

    def one_block(args):
        qb, qp = args
        s = jnp.einsum('bqhd,bkhd->bhqk', qb, k).astype(jnp.float32) * scale
        s = jnp.where(k_pos[None, :] <= qp[:, None], s, -jnp.inf)
        p = jax.nn.softmax(s, axis=-1)
        return jnp.einsum('bhqk,bkhd->bqhd', p, v)

    o = lax.map(one_block, (q_blocks, q_pos))
    return o.swapaxes(0, 1).reshape(b, t, N_HEADS, V_HEAD).astype(q.dtype)


def _attn_sample(q, kv_new, kr_new, cache_kv, cache_kr, page_table, w_uk, w_uv, g_k_head):
    b, s = q.shape[0], q.shape[1]
    scale = QK_DIM ** -0.5
    qf = q.astype(jnp.float32)

    def absorb(carry, kv_blk, kr_blk, mask):
        m, l, acc = carry
        k = _mla_keys(kv_blk, kr_blk, w_uk, g_k_head).astype(jnp.float32)
        sc = jnp.einsum('bshd,bphd->bhsp', qf, k) * scale
        if mask is not None:
            sc = jnp.where(mask, sc, -jnp.inf)
        m_new = jnp.maximum(m, jnp.max(sc, axis=-1))
        corr = jnp.exp(m - m_new)
        p = jnp.exp(sc - m_new[..., None])
        l = l * corr + jnp.sum(p, axis=-1)
        acc = acc * corr[..., None] + jnp.einsum('bhsp,bpl->bhsl', p, kv_blk.astype(jnp.float32))
        return (m_new, l, acc)

    def page_step(carry, pages):
        return absorb(carry, cache_kv[pages], cache_kr[pages], None), None

    init = (jnp.full((b, N_HEADS, s), -jnp.inf, jnp.float32),
            jnp.zeros((b, N_HEADS, s), jnp.float32),
            jnp.zeros((b, N_HEADS, s, KV_LORA), jnp.float32))
    carry, _ = lax.scan(page_step, init, page_table.T)
    causal = jnp.arange(s)[None, :] <= jnp.arange(s)[:, None]
    m, l, acc = absorb(carry, kv_new, kr_new, causal[None, None])
    o_lat = acc / l[..., None]
    return jnp.einsum('bhsl,lhd->bshd', o_lat, w_uv.astype(jnp.float32)).astype(q.dtype)


def _conv_branch(glu_in, buf_prev, w_dw, b_dw, g_cn, b_cn, w_conv_out):
    za, zb = jnp.split(glu_in, 2, axis=-1)
    z = za * jax.nn.sigmoid(zb)
    buf = jnp.concatenate([buf_prev.astype(z.dtype), z], axis=1)
    y = lax.conv_general_dilated(buf, w_dw[:, None, :].astype(buf.dtype), window_strides=(1,),
                                 padding='VALID', dimension_numbers=('NWC', 'WIO', 'NWC'),
                                 feature_group_count=CONV_CH) + b_dw
    y = jax.nn.silu(_layernorm(y, g_cn, b_cn))
    return y @ w_conv_out, buf[:, -CONV_BUF:, :]


def _decoder_layer(x, c, pos, attn_fn, conv_buf, wts):
    sf1, cf1, gf1, sm, cm, gm, sf2, cf2, gf2 = _adaln(c, wts['w_ada'], wts['b_ada'])
    h = x + 0.5 * gf1 * _swiglu(_modulate(_rmsnorm(x, wts['g_norm_ffn1']), sf1, cf1),
                                wts['w_ffn1_in'], wts['w_ffn1_out'])
    u = _modulate(_rmsnorm(h, wts['g_norm_mix']), sm, cm)
    q_a, kv_a, k_pe, glu_in, gate_logits = _split_in(u @ wts['w_in'])
    q = _mla_query(q_a, pos, wts['g_q_lat'], wts['w_uq'], wts['g_q_head'])
    kv_lat, k_rope = _mla_latent(kv_a, k_pe, pos, wts['g_kv_lat'])
    o = attn_fn(q, kv_lat, k_rope)
    a_out = jnp.einsum('bthd,hdo->bto', o, wts['w_attn_out'])
    b_out, new_buf = _conv_branch(glu_in, conv_buf, wts['w_dw'], wts['b_dw'],
                                  wts['g_conv_norm'], wts['b_conv_norm'], wts['w_conv_out'])
    g_a, g_b = jnp.split(gate_logits, N_BRANCH, axis=-1)
    merged = jax.nn.sigmoid(g_a) * a_out + jax.nn.sigmoid(g_b) * b_out
    h = h + gm * (merged @ wts['w_out'])
    y = h + 0.5 * gf2 * _swiglu(_modulate(_rmsnorm(h, wts['g_norm_ffn2']), sf2, cf2),
                                wts['w_ffn2_in'], wts['w_ffn2_out'])
    return y, kv_lat, k_rope, new_buf


def setup_inputs(seed: int = 0) -> dict:
    key = jax.random.key(seed)
    ks = jax.random.split(key, 40)
    f32 = jnp.float32
    n_pages = PAST_LEN // PAGE_SIZE
    n_used = DEC_BATCH * n_pages
    n_pool = n_used + max(1, n_used // 4)

    def w(k, shape, fan_in, mult=1.0):
        return jax.random.normal(k, shape, f32) * (mult * fan_in ** -0.5)

    def gain(k, n):
        return 1.0 + 0.1 * jax.random.normal(k, (n,), f32)

    def bias(k, n):
        return 0.02 * jax.random.normal(k, (n,), f32)

    perm = jax.random.permutation(ks[7], n_pool)[:n_used]
    page_table = perm.reshape(DEC_BATCH, n_pages).astype(jnp.int32)
    return {
        'x_prompt': jax.random.normal(ks[0], (BATCH, SEQ, D_MODEL), f32),
        'x_sample': jax.random.normal(ks[1], (DEC_BATCH, DEC_SEQ, D_MODEL), f32),
        'c_prompt': jax.random.normal(ks[2], (BATCH, D_MODEL), f32),
        'c_sample': jax.random.normal(ks[3], (DEC_BATCH, D_MODEL), f32),
        'cache_kv_latent': jax.random.normal(ks[4], (n_pool, PAGE_SIZE, KV_LORA), f32),
        'cache_k_rope': jax.random.normal(ks[5], (n_pool, PAGE_SIZE, QK_ROPE), f32),
        'state_conv': 0.5 * jax.random.normal(ks[6], (DEC_BATCH, CONV_BUF, CONV_CH), f32),
        'page_table': page_table,
        'w_ada': w(ks[8], (D_MODEL, N_MOD * D_MODEL), D_MODEL, 0.5),
        'b_ada': bias(ks[9], N_MOD * D_MODEL),
        'g_norm_ffn1': gain(ks[10], D_MODEL),
        'w_ffn1_in': w(ks[11], (D_MODEL, 2 * D_FF), D_MODEL),
        'w_ffn1_out': w(ks[12], (D_FF, D_MODEL), D_FF),
        'g_norm_mix': gain(ks[13], D_MODEL),
        'w_in': w(ks[14], (D_MODEL, IN_COLS), D_MODEL),
        'g_q_lat': gain(ks[15], Q_LORA),
        'w_uq': w(ks[16], (Q_LORA, N_HEADS, QK_DIM), Q_LORA),
        'g_q_head': gain(ks[17], QK_DIM),
        'g_kv_lat': gain(ks[18], KV_LORA),
        'w_uk': w(ks[19], (KV_LORA, N_HEADS, QK_NOPE), KV_LORA),
        'w_uv': w(ks[20], (KV_LORA, N_HEADS, V_HEAD), KV_LORA),
        'g_k_head': gain(ks[21], QK_DIM),
        'w_attn_out': w(ks[22], (N_HEADS, V_HEAD, D_MODEL), N_HEADS * V_HEAD),
        'w_dw': w(ks[23], (CONV_WIDTH, CONV_CH), CONV_WIDTH),
        'b_dw': bias(ks[24], CONV_CH),
        'g_conv_norm': gain(ks[25], CONV_CH),
        'b_conv_norm': bias(ks[26], CONV_CH),
        'w_conv_out': w(ks[27], (CONV_CH, D_MODEL), CONV_CH),
        'w_out': w(ks[28], (D_MODEL, D_MODEL), D_MODEL),
        'g_norm_ffn2': gain(ks[29], D_MODEL),
        'w_ffn2_in': w(ks[30], (D_MODEL, 2 * D_FF), D_MODEL),
        'w_ffn2_out': w(ks[31], (D_FF, D_MODEL), D_FF),
    }


def reference(x_prompt, x_sample, c_prompt, c_sample, cache_kv_latent, cache_k_rope, state_conv,
              page_table, w_ada, b_ada, g_norm_ffn1, w_ffn1_in, w_ffn1_out, g_norm_mix, w_in,
              g_q_lat, w_uq, g_q_head, g_kv_lat, w_uk, w_uv, g_k_head, w_attn_out, w_dw, b_dw,
              g_conv_norm, b_conv_norm, w_conv_out, w_out, g_norm_ffn2, w_ffn2_in, w_ffn2_out):
    wts = dict(w_ada=w_ada, b_ada=b_ada, g_norm_ffn1=g_norm_ffn1, w_ffn1_in=w_ffn1_in,
               w_ffn1_out=w_ffn1_out, g_norm_mix=g_norm_mix, w_in=w_in, g_q_lat=g_q_lat,
               w_uq=w_uq, g_q_head=g_q_head, g_kv_lat=g_kv_lat, w_attn_out=w_attn_out,
               w_dw=w_dw, b_dw=b_dw, g_conv_norm=g_conv_norm, b_conv_norm=b_conv_norm,
               w_conv_out=w_conv_out, w_out=w_out, g_norm_ffn2=g_norm_ffn2,
               w_ffn2_in=w_ffn2_in, w_ffn2_out=w_ffn2_out)
    pos_prompt = jnp.arange(x_prompt.shape[1], dtype=jnp.int32)
    pos_sample = jnp.arange(x_sample.shape[1], dtype=jnp.int32) + PAST_LEN

    def attn_prompt(q, kv_lat, k_rope):
        return _attn_prompt(q, kv_lat, k_rope, w_uk, w_uv, g_k_head)

    def attn_sample(q, kv_lat, k_rope):
        return _attn_sample(q, kv_lat, k_rope, cache_kv_latent, cache_k_rope, page_table,
                            w_uk, w_uv, g_k_head)

    conv_zero = jnp.zeros((x_prompt.shape[0], CONV_BUF, CONV_CH), x_prompt.dtype)
    y_prompt, y_sample = x_prompt, x_sample
    for _ in range(DEPTH):
        y_prompt, kv_latent_prompt, k_rope_prompt, conv_state_prompt = _decoder_layer(
            y_prompt, c_prompt, pos_prompt, attn_prompt, conv_zero, wts)
        y_sample, kv_latent_sample, k_rope_sample, conv_state_sample = _decoder_layer(
            y_sample, c_sample, pos_sample, attn_sample, state_conv, wts)
    return (y_prompt, y_sample, kv_latent_prompt, k_rope_prompt, conv_state_prompt,
            kv_latent_sample, k_rope_sample, conv_state_sample)
```

```python
import functools

import jax
import jax.numpy as jnp
from jax import lax
from jax.experimental import pallas as pl
from jax.experimental.pallas import tpu as pltpu

F32 = jnp.float32
BF16 = jnp.bfloat16
EPS = 1e-6
ROPE_THETA = 10000.0
LANE = 128
SLAB = 128
CONV_PAD = 32
VMEM_LIMIT_BYTES = 56 * 1024 * 1024


def _cparams(*sem):
    return pltpu.CompilerParams(dimension_semantics=sem, vmem_limit_bytes=VMEM_LIMIT_BYTES)


def _dot(a, b):
    return jnp.dot(a, b, preferred_element_type=F32)


def _dot_nt(a, b):
    return lax.dot_general(a, b, (((1,), (1,)), ((), ())), preferred_element_type=F32)


def _rms(x, g):
    return x * lax.rsqrt(jnp.mean(x * x, axis=-1, keepdims=True) + EPS) * g


def _silu(x):
    return x * jax.nn.sigmoid(x)


def _hilo(x):
    hi = x.astype(BF16)
    lo = (x - hi.astype(F32)).astype(BF16)
    return hi, lo


def _const_spec(shape):
    nd = len(shape)
    return pl.BlockSpec(shape, lambda *_: (0,) * nd, pipeline_mode=pl.Buffered(1))


def _ada_kernel(c_ref, w_ref, b_ref, o_ref):
    a = _silu(c_ref[...]).astype(BF16)
    o_ref[...] = _dot(a, w_ref[...].astype(BF16)) + b_ref[...]


def _ada(c_all, w_ada, b_ada):
    m, d = c_all.shape
    n = w_ada.shape[1]
    tn = n // 8
    return pl.pallas_call(
        _ada_kernel,
        grid=(n // tn,),
        in_specs=[pl.BlockSpec((m, d), lambda j: (0, 0)),
                  pl.BlockSpec((d, tn), lambda j: (0, j)),
                  pl.BlockSpec((1, tn), lambda j: (0, j))],
        out_specs=pl.BlockSpec((m, tn), lambda j: (0, j)),
        out_shape=jax.ShapeDtypeStruct((m, n), F32),
        compiler_params=_cparams("arbitrary"),
        name="ada",
    )(c_all, w_ada, b_ada.reshape(1, n))


def _rope_kernel(inv_ref, cq_ref, sq_ref, ck_ref, sk_ref, *, pos0):
    t = cq_ref.shape[0]
    pos = (lax.broadcasted_iota(jnp.int32, (t, LANE), 0) + pos0).astype(F32)
    aq = pos * inv_ref[0:1, :]
    ak = pos * inv_ref[1:2, :]
    cq_ref[...] = jnp.cos(aq)
    sq_ref[...] = jnp.sin(aq)
    ck_ref[...] = jnp.cos(ak)
    sk_ref[...] = jnp.sin(ak)


def _rope_tables(inv2, t, pos0):
    shp = jax.ShapeDtypeStruct((t, LANE), F32)
    return pl.pallas_call(
        functools.partial(_rope_kernel, pos0=pos0),
        out_shape=(shp, shp, shp, shp),
        name="rope_tables",
    )(inv2)


def _ffn_kernel(x_ref, sh_ref, sc_ref, gate_ref, g_ref, win_ref, wout_ref, o_ref, *, f, fc):
    x = x_ref[...]
    u = (_rms(x, g_ref[...]) * (1.0 + sc_ref[...]) + sh_ref[...]).astype(BF16)
    acc = None
    for c in range(f // fc):
        up = _dot(u, win_ref[:, c * fc:(c + 1) * fc])
        gt = _dot(u, win_ref[:, f + c * fc:f + (c + 1) * fc])
        a = (_silu(gt) * up).astype(BF16)
        part = _dot(a, wout_ref[c * fc:(c + 1) * fc, :])
        acc = part if acc is None else acc + part
    o_ref[...] = x + 0.5 * gate_ref[...] * acc


def _mod_spec(arr, tiles_per_seq):
    r, d = arr.shape[1:]
    return pl.BlockSpec((None, r, d), lambda i: (i // tiles_per_seq, 0, 0))


def _ffn(x, shift, scale, gate, g, w_in, w_out, tm, tiles_per_seq):
    m, d = x.shape
    f = w_out.shape[0]
    fc = f // 2
    return pl.pallas_call(
        functools.partial(_ffn_kernel, f=f, fc=fc),
        grid=(m // tm,),
        in_specs=[pl.BlockSpec((tm, d), lambda i: (i, 0)),
                  _mod_spec(shift, tiles_per_seq), _mod_spec(scale, tiles_per_seq),
                  _mod_spec(gate, tiles_per_seq),
                  _const_spec((1, d)), _const_spec(w_in.shape), _const_spec(w_out.shape)],
        out_specs=pl.BlockSpec((tm, d), lambda i: (i, 0)),
        out_shape=jax.ShapeDtypeStruct((m, d), F32),
        compiler_params=_cparams("arbitrary"),
        name="ffn",
    )(x, shift, scale, gate, g, w_in, w_out)


def _head_norm_store(x, gain, o_ref, n_heads, d_head):
    for h in range(n_heads):
        xh = x[:, h * SLAB:(h + 1) * SLAB]
        ss = jnp.sum(xh * xh, axis=-1, keepdims=True)
        rinv = lax.rsqrt(ss * (1.0 / d_head) + EPS)
        o_ref[:, h * SLAB:(h + 1) * SLAB] = (xh * rinv * gain).astype(o_ref.dtype)


def _q_heads(qn, wuq_ref, cq, sq, gain, o_ref, n_heads, d_head):
    hw = n_heads * SLAB
    group = 4
    for hc in range(n_heads // group):
        lo_, hi_ = hc * group * SLAB, (hc + 1) * group * SLAB
        a = _dot(qn, wuq_ref[:, lo_:hi_])
        r = _dot(qn, wuq_ref[:, hw + lo_:hw + hi_])
        for j in range(group):
            h = hc * group + j
            qh = a[:, j * SLAB:(j + 1) * SLAB] * cq + r[:, j * SLAB:(j + 1) * SLAB] * sq
            ss = jnp.sum(qh * qh, axis=-1, keepdims=True)
            rinv = lax.rsqrt(ss * (1.0 / d_head) + EPS)
            o_ref[:, h * SLAB:(h + 1) * SLAB] = (qh * rinv * gain).astype(o_ref.dtype)


def _latent_paths(u, wkv_ref, gkv_ref, ck, sk, kv_lora, rope):
    kvpe = _dot(u, wkv_ref[...])
    kv_lat = _rms(kvpe[:, :kv_lora], gkv_ref[...])
    kr = kvpe[:, kv_lora:kv_lora + LANE] * ck + kvpe[:, kv_lora + LANE:kv_lora + 2 * LANE] * sk
    return kv_lat, kr


def _key_ext(kv_lat_bf, kr, wuk_ref):
    kr_hi, kr_lo = _hilo(kr)
    x = jnp.concatenate([kv_lat_bf, kr_hi, kr_lo], axis=-1)
    return _dot(x, wuk_ref[...])


def _mixp_kernel(h_ref, sh_ref, sc_ref, gmix_ref, wq_ref, gql_ref, wuq_ref, gqh_ref,
                 wkv_ref, gkv_ref, wuk_ref, gkh_ref, wuv_ref, wglu_ref,
                 cq_ref, sq_ref, ck_ref, sk_ref, wdw_ref, bdw_ref, gcn_ref, bcn_ref,
                 kvlat_ref, krope_ref, qx_ref, kx_ref, v_ref, ycv_ref, cst_ref,
                 zb_ref, y_ref, *, n_heads, d_head, kv_lora, rope, conv_w, conv_ch):
    t = pl.program_id(1)
    nt = pl.num_programs(1)
    tm = h_ref.shape[0]
    u = (_rms(h_ref[...], gmix_ref[...]) * (1.0 + sc_ref[...]) + sh_ref[...]).astype(BF16)

    qn = _rms(_dot(u, wq_ref[...]), gql_ref[...]).astype(BF16)
    _q_heads(qn, wuq_ref, cq_ref[...], sq_ref[...], gqh_ref[...], qx_ref, n_heads, d_head)

    kv_lat, kr = _latent_paths(u, wkv_ref, gkv_ref, ck_ref[...], sk_ref[...], kv_lora, rope)
    kvlat_ref[...] = kv_lat
    krope_ref[...] = kr[:, :rope]
    kv_bf = kv_lat.astype(BF16)
    _head_norm_store(_key_ext(kv_bf, kr, wuk_ref), gkh_ref[...], kx_ref, n_heads, d_head)
    v_ref[...] = _dot(kv_bf, wuv_ref[...]).astype(BF16)

    glu = _dot(u, wglu_ref[...])
    z = glu[:, :conv_ch] * jax.nn.sigmoid(glu[:, conv_ch:])

    @pl.when(t == 0)
    def _():
        zb_ref[0:CONV_PAD, :] = jnp.zeros((CONV_PAD, conv_ch), F32)

    zb_ref[CONV_PAD:CONV_PAD + tm, :] = z
    off = CONV_PAD - (conv_w - 1)
    rb, cw = 8, 512

    def conv_rows(i, carry):
        r0 = pl.multiple_of(i * rb, rb)
        for cc in range(conv_ch // cw):
            cs = slice(cc * cw, (cc + 1) * cw)
            win = zb_ref[pl.ds(r0, rb + CONV_PAD), cs]
            acc = jnp.broadcast_to(bdw_ref[:, cs], (rb, cw))
            for k in range(conv_w):
                acc = acc + win[off + k:off + k + rb, :] * wdw_ref[k:k + 1, cs]
            y_ref[pl.ds(r0, rb), cs] = acc
        return carry

    lax.fori_loop(0, tm // rb, conv_rows, 0)

    y = y_ref[...]
    mu = jnp.mean(y, axis=-1, keepdims=True)
    yc = y - mu
    var = jnp.mean(yc * yc, axis=-1, keepdims=True)
    yn = yc * lax.rsqrt(var + EPS) * gcn_ref[...] + bcn_ref[...]
    ycv_ref[...] = _silu(yn).astype(BF16)

    tail = zb_ref[tm:tm + CONV_PAD, :]
    zb_ref[0:CONV_PAD, :] = tail

    @pl.when(t == nt - 1)
    def _():
        cst_ref[...] = tail


def _mixer_prompt(h, shift, scale, wts, tabs, dims, tm):
    b, t, d = h.shape
    nh, dh, kvl, rope, cw_, cch = dims
    nt = t // tm
    hw = nh * SLAB
    kern = functools.partial(_mixp_kernel, n_heads=nh, d_head=dh, kv_lora=kvl, rope=rope,
                             conv_w=cw_, conv_ch=cch)
    tok = lambda n: pl.BlockSpec((None, tm, n), lambda i, j: (i, j, 0))
    mod = pl.BlockSpec((None, 1, d), lambda i, j: (i, 0, 0))
    tab = pl.BlockSpec((tm, LANE), lambda i, j: (j, 0))
    wnames = ("g_mix", "wq", "g_q_lat", "wuq", "gqh", "wkv", "g_kv_lat", "wuk", "gkh", "wuv", "wglu")
    cnames = ("wdw", "b_dw", "g_cn", "b_cn")
    in_specs = ([tok(d), mod, mod] + [_const_spec(wts[n].shape) for n in wnames] + [tab] * 4
                + [_const_spec(wts[n].shape) for n in cnames])
    out_shape = (jax.ShapeDtypeStruct((b, t, kvl), F32), jax.ShapeDtypeStruct((b, t, rope), F32),
                 jax.ShapeDtypeStruct((b, t, hw), BF16), jax.ShapeDtypeStruct((b, t, hw), BF16),
                 jax.ShapeDtypeStruct((b, t, cch), BF16), jax.ShapeDtypeStruct((b, t, cch), BF16),
                 jax.ShapeDtypeStruct((b, CONV_PAD, cch), F32))
    out_specs = (tok(kvl), tok(rope), tok(hw), tok(hw), tok(cch), tok(cch),
                 pl.BlockSpec((None, CONV_PAD, cch), lambda i, j: (i, 0, 0)))
    return pl.pallas_call(
        kern, grid=(b, nt), in_specs=in_specs, out_specs=out_specs, out_shape=out_shape,
        scratch_shapes=[pltpu.VMEM((tm + CONV_PAD, cch), F32), pltpu.VMEM((tm, cch), F32)],
        compiler_params=_cparams("arbitrary", "arbitrary"),
        name="mixer_prompt",
    )(h, shift, scale, *[wts[n] for n in wnames], *tabs, *[wts[n] for n in cnames])


def _attn_kernel(q_ref, k_ref, v_ref, o_ref, *, tq, d_v):
    t = q_ref.shape[0]
    row = lax.broadcasted_iota(jnp.int32, (tq, tq), 0)
    col = lax.broadcasted_iota(jnp.int32, (tq, tq), 1)
    causal = col <= row
    first = lax.broadcasted_iota(jnp.int32, (tq, LANE), 1) < d_v
    for qi in range(t // tq):
        rows = slice(qi * tq, (qi + 1) * tq)
        outs = []
        for hh in range(2):
            lanes = slice(hh * SLAB, (hh + 1) * SLAB)
            q = q_ref[rows, lanes]
            sd = jnp.where(causal, _dot_nt(q, k_ref[rows, lanes]), -jnp.inf)
            m = jnp.max(sd, axis=-1, keepdims=True)
            if qi > 0:
                so = _dot_nt(q, k_ref[0:qi * tq, lanes])
                m = jnp.maximum(m, jnp.max(so, axis=-1, keepdims=True))
            pd = jnp.exp(sd - m)
            l = jnp.sum(pd, axis=-1, keepdims=True)
            o = _dot(pd.astype(BF16), v_ref[rows, :])
            if qi > 0:
                po = jnp.exp(so - m)
                l = l + jnp.sum(po, axis=-1, keepdims=True)
                o = o + _dot(po.astype(BF16), v_ref[0:qi * tq, :])
            outs.append(o / l)
        o_ref[rows, :] = jnp.where(first, outs[0], outs[1]).astype(o_ref.dtype)


def _attn_prompt(qx, kx, v, n_heads, d_v, tq):
    b, t, _ = qx.shape
    pairs = n_heads // 2
    return pl.pallas_call(
        functools.partial(_attn_kernel, tq=tq, d_v=d_v),
        grid=(b, pairs),
        in_specs=[pl.BlockSpec((None, t, 2 * SLAB), lambda i, p: (i, 0, p)),
                  pl.BlockSpec((None, t, 2 * SLAB), lambda i, p: (i, 0, p)),
                  pl.BlockSpec((None, t, 2 * d_v), lambda i, p: (i, 0, p))],
        out_specs=pl.BlockSpec((None, t, 2 * d_v), lambda i, p: (i, 0, p)),
        out_shape=jax.ShapeDtypeStruct((b, t, n_heads * d_v), BF16),
        compiler_params=_cparams("arbitrary", "arbitrary"),
        name="attn_prompt",
    )(qx, kx, v)


def _merge_kernel(h_ref, sh_ref, sc_ref, gate_ref, g_ref, o_ref, y_ref,
                  wg_ref, wao_ref, wco_ref, wout_ref, out_ref):
    h = h_ref[...]
    u = (_rms(h, g_ref[...]) * (1.0 + sc_ref[...]) + sh_ref[...]).astype(BF16)
    gl = _dot(u, wg_ref[...])
    d = wao_ref.shape[1]
    a = _dot(o_ref[...], wao_ref[...])
    b = _dot(y_ref[...], wco_ref[...])
    merged = jax.nn.sigmoid(gl[:, :d]) * a + jax.nn.sigmoid(gl[:, d:]) * b
    out_ref[...] = h + gate_ref[...] * _dot(merged.astype(BF16), wout_ref[...])


def _merge(h, shift, scale, gate, g, o, ycv, wg, wao, wco, wout, tm, tiles_per_seq):
    m, d = h.shape
    row = lambda n: pl.BlockSpec((tm, n), lambda i: (i, 0))
    return pl.pallas_call(
        _merge_kernel,
        grid=(m // tm,),
        in_specs=[row(d), _mod_spec(shift, tiles_per_seq), _mod_spec(scale, tiles_per_seq),
                  _mod_spec(gate, tiles_per_seq), _const_spec((1, d)), row(o.shape[1]), row(ycv.shape[1]),
                  _const_spec(wg.shape), _const_spec(wao.shape), _const_spec(wco.shape),
                  _const_spec(wout.shape)],
        out_specs=row(d),
        out_shape=jax.ShapeDtypeStruct((m, d), F32),
        compiler_params=_cparams("arbitrary"),
        name="merge",
    )(h, shift, scale, gate, g, o, ycv, wg, wao, wco, wout)


def _mixs_kernel(h_ref, sh_ref, sc_ref, gmix_ref, wq_ref, gql_ref, wuq_ref, gqh_ref,
                 wkv_ref, gkv_ref, wuk_ref, gkh_ref, wukt_ref, wglu_ref,
                 cq_ref, sq_ref, ck_ref, sk_ref, wdw_ref, bdw_ref, gcn_ref, bcn_ref, st_ref,
                 kvlat_ref, krope_ref, qx_ref, qg_ref, qt_ref, knew_ref, z_ref, ycv_ref,
                 *, n_heads, d_head, kv_lora, rope, conv_w, conv_ch):
    u = (_rms(h_ref[...], gmix_ref[...]) * (1.0 + sc_ref[...]) + sh_ref[...]).astype(BF16)

    qn = _rms(_dot(u, wq_ref[...]), gql_ref[...]).astype(BF16)
    _q_heads(qn, wuq_ref, cq_ref[0:1, :], sq_ref[0:1, :], gqh_ref[...], qx_ref, n_heads, d_head)
    for h in range(n_heads):
        qg = qx_ref[:, h * SLAB:(h + 1) * SLAB] * gkh_ref[...]
        qg_ref[:, h * SLAB:(h + 1) * SLAB] = qg
        qt_ref[:, h * kv_lora:(h + 1) * kv_lora] = _dot(qg.astype(BF16), wukt_ref[h])

    kv_lat, kr = _latent_paths(u, wkv_ref, gkv_ref, ck_ref[0:1, :], sk_ref[0:1, :], kv_lora, rope)
    kvlat_ref[...] = kv_lat
    krope_ref[...] = kr[:, :rope]
    _head_norm_store(_key_ext(kv_lat.astype(BF16), kr, wuk_ref), gkh_ref[...], knew_ref, n_heads, d_head)

    glu = _dot(u, wglu_ref[...])
    z = glu[:, :conv_ch] * jax.nn.sigmoid(glu[:, conv_ch:])
    z_ref[...] = z
    hist = jnp.sum(st_ref[...] * wdw_ref[0:conv_w - 1, :][None, :, :], axis=1)
    y = hist + z * wdw_ref[conv_w - 1:conv_w, :] + bdw_ref[...]
    mu = jnp.mean(y, axis=-1, keepdims=True)
    yc = y - mu
    var = jnp.mean(yc * yc, axis=-1, keepdims=True)
    yn = yc * lax.rsqrt(var + EPS) * gcn_ref[...] + bcn_ref[...]
    ycv_ref[...] = _silu(yn).astype(BF16)


def _mixer_sample(h, shift, scale, wts, tabs, state, dims, tm):
    m, d = h.shape
    nh, dh, kvl, rope, cw_, cch = dims
    hw = nh * SLAB
    kern = functools.partial(_mixs_kernel, n_heads=nh, d_head=dh, kv_lora=kvl, rope=rope,
                             conv_w=cw_, conv_ch=cch)
    row = lambda n: pl.BlockSpec((tm, n), lambda i: (i, 0))
    tab = pl.BlockSpec((8, LANE), lambda i: (0, 0))
    wnames = ("g_mix", "wq", "g_q_lat", "wuq", "gqh", "wkv", "g_kv_lat", "wuk", "gkh", "wukt", "wglu")
    cnames = ("wdw", "b_dw", "g_cn", "b_cn")
    in_specs = ([row(d), row(d), row(d)] + [_const_spec(wts[n].shape) for n in wnames] + [tab] * 4
                + [_const_spec(wts[n].shape) for n in cnames]
                + [pl.BlockSpec((tm, cw_ - 1, cch), lambda i: (i, 0, 0))])
    out_shape = (jax.ShapeDtypeStruct((m, kvl), F32), jax.ShapeDtypeStruct((m, rope), F32),
                 jax.ShapeDtypeStruct((m, hw), F32), jax.ShapeDtypeStruct((m, hw), F32),
                 jax.ShapeDtypeStruct((m, nh * kvl), F32), jax.ShapeDtypeStruct((m, hw), F32),
                 jax.ShapeDtypeStruct((m, cch), F32), jax.ShapeDtypeStruct((m, cch), BF16))
    out_specs = (row(kvl), row(rope), row(hw), row(hw), row(nh * kvl), row(hw), row(cch), row(cch))
    return pl.pallas_call(
        kern, grid=(m // tm,), in_specs=in_specs, out_specs=out_specs, out_shape=out_shape,
        compiler_params=_cparams("arbitrary"),
        name="mixer_sample",
    )(h, shift, scale, *[wts[n] for n in wnames], *tabs, *[wts[n] for n in cnames], state)


def _paged_kernel(pt_ref, qg_ref, qt_ref, qx_ref, knew_ref, cnew_ref, wukt_ref, kv_hbm, kr_hbm,
                  o_ref, kvbuf, krbuf, sem, cbf, s_scr, m_scr, l_scr, acc_scr,
                  *, pages, page, tile, n_heads, d_head, d_nope, rope):
    b = pl.program_id(0)
    c = pl.program_id(1)
    nc = pl.num_programs(1)
    step = b * nc + c
    nsteps = pl.num_programs(0) * nc
    slot = step % 2
    npos = pages * page

    def copies(step_i, slot_i):
        out = []
        for p in range(pages):
            pg = pt_ref[step_i * pages + p]
            out.append(pltpu.make_async_copy(
                kv_hbm.at[pg], kvbuf.at[slot_i, pl.ds(p * page, page), :], sem.at[0, slot_i]))
            out.append(pltpu.make_async_copy(
                kr_hbm.at[pg], krbuf.at[slot_i, pl.ds(p * page, page), :], sem.at[1, slot_i]))
        return out

    @pl.when(step == 0)
    def _():
        for cp in copies(0, 0):
            cp.start()

    @pl.when(step + 1 < nsteps)
    def _():
        for cp in copies(step + 1, 1 - slot):
            cp.start()

    for cp in copies(step, slot):
        cp.wait()

    @pl.when(c == 0)
    def _():
        m_scr[...] = jnp.full(m_scr.shape, -jnp.inf, F32)
        l_scr[...] = jnp.zeros(l_scr.shape, F32)
        acc_scr[...] = jnp.zeros(acc_scr.shape, F32)

    qt_hi, qt_lo = _hilo(qt_ref[...])
    qtl = jnp.concatenate([qt_hi, qt_lo], axis=0)
    qg_hi, qg_lo = _hilo(qg_ref[:, d_nope:d_head])
    qgl = jnp.concatenate([qg_hi, qg_lo], axis=0)

    krt = krbuf[slot].T
    krss = jnp.sum(krt * krt, axis=0, keepdims=True)
    sr = _dot(qgl, krt.astype(BF16))
    s_rope = sr[:n_heads] + sr[n_heads:]

    for j in range(npos // tile):
        cols = slice(j * tile, (j + 1) * tile)
        cb = kvbuf[slot, cols, :].astype(BF16)
        cbf[cols, :] = cb
        kn = _dot_nt(wukt_ref[...], cb)
        ssn = jnp.sum((kn * kn).reshape(n_heads, d_nope, tile), axis=1)
        sn = _dot_nt(qtl, cb)
        rinv = lax.rsqrt((ssn + krss[:, cols]) * (1.0 / d_head) + EPS)
        s_scr[:, cols] = (sn[:n_heads] + sn[n_heads:] + s_rope[:, cols]) * rinv

    s = s_scr[...]
    m_prev = m_scr[...]
    m_new = jnp.maximum(m_prev, jnp.max(s, axis=-1, keepdims=True))
    corr = jnp.exp(m_prev - m_new)
    p = jnp.exp(s - m_new)
    l_new = l_scr[...] * corr + jnp.sum(p, axis=-1, keepdims=True)
    acc_new = acc_scr[...] * corr + _dot(p.astype(BF16), cbf[...])
    m_scr[...] = m_new
    l_scr[...] = l_new
    acc_scr[...] = acc_new

    @pl.when(c == nc - 1)
    def _():
        s_self = jnp.sum(qx_ref[...] * knew_ref[...], axis=-1, keepdims=True)
        m2 = jnp.maximum(m_new, s_self)
        corr2 = jnp.exp(m_new - m2)
        p_self = jnp.exp(s_self - m2)
        l2 = l_new * corr2 + p_self
        o_ref[...] = (acc_new * corr2 + p_self * cnew_ref[...]) / l2


def _paged_attn(page_table, qg, qt, qx, knew, cnew, wukt, cache_kv, cache_kr, dims, pages, tile):
    nb, n_pages = page_table.shape
    nh, dh, kvl, rope = dims
    d_nope = dh - rope
    page = cache_kv.shape[1]
    nc = n_pages // pages
    npos = pages * page
    kern = functools.partial(_paged_kernel, pages=pages, page=page, tile=tile, n_heads=nh,
                             d_head=dh, d_nope=d_nope, rope=rope)
    per_b = lambda r, n: pl.BlockSpec((None, r, n), lambda i, j, pt: (i, 0, 0))
    grid_spec = pltpu.PrefetchScalarGridSpec(
        num_scalar_prefetch=1,
        grid=(nb, nc),
        in_specs=[per_b(nh, SLAB), per_b(nh, kvl), per_b(nh, SLAB), per_b(nh, SLAB), per_b(1, kvl),
                  pl.BlockSpec(wukt.shape, lambda i, j, pt: (0, 0)),
                  pl.BlockSpec(memory_space=pl.ANY), pl.BlockSpec(memory_space=pl.ANY)],
        out_specs=per_b(nh, kvl),
        scratch_shapes=[pltpu.VMEM((2, npos, kvl), F32), pltpu.VMEM((2, npos, rope), F32),
                        pltpu.SemaphoreType.DMA((2, 2)),
                        pltpu.VMEM((npos, kvl), BF16), pltpu.VMEM((nh, npos), F32),
                        pltpu.VMEM((nh, 1), F32), pltpu.VMEM((nh, 1), F32), pltpu.VMEM((nh, kvl), F32)])
    return pl.pallas_call(
        kern, grid_spec=grid_spec,
        out_shape=jax.ShapeDtypeStruct((nb, nh, kvl), F32),
        compiler_params=_cparams("arbitrary", "arbitrary"),
        name="paged_attn",
    )(page_table.reshape(-1), qg, qt, qx, knew, cnew, wukt, cache_kv, cache_kr)


def _uvup_kernel(ol_ref, wa_ref, wb_ref, o_ref):
    for p in range(wa_ref.shape[0]):
        a = _dot(ol_ref[2 * p].astype(BF16), wa_ref[p])
        b = _dot(ol_ref[2 * p + 1].astype(BF16), wb_ref[p])
        o_ref[:, p * LANE:(p + 1) * LANE] = (a + b).astype(o_ref.dtype)


def _uv_up(olat_t, wa, wb):
    nh, m, _ = olat_t.shape
    return pl.pallas_call(
        _uvup_kernel,
        out_shape=jax.ShapeDtypeStruct((m, wa.shape[0] * LANE), BF16),
        compiler_params=pltpu.CompilerParams(vmem_limit_bytes=VMEM_LIMIT_BYTES),
        name="uv_up",
    )(olat_t, wa, wb)


def _prep_weights(w_in, g_norm_mix, g_q_lat, w_uq, g_q_head, g_kv_lat, w_uk, w_uv, g_k_head,
                  w_dw, b_dw, g_conv_norm, b_conv_norm):
    d = w_in.shape[0]
    q_lora, nh, dh = w_uq.shape
    kvl, _, d_nope = w_uk.shape
    rope = dh - d_nope
    half = rope // 2
    conv_w, cch = w_dw.shape
    pad = SLAB - dh
    c0, c1, c2, c3 = q_lora, q_lora + kvl, q_lora + kvl + rope, q_lora + kvl + rope + 2 * cch

    def rot_half(x):
        return jnp.concatenate([-x[..., half:], x[..., :half]], axis=-1)

    pe = w_in[:, c1:c2]
    zpad = jnp.zeros((d, LANE - rope), F32)
    wkv = jnp.concatenate([w_in[:, c0:c1], pe, zpad, rot_half(pe), zpad], axis=1)

    wuq_a = jnp.pad(w_uq, ((0, 0), (0, 0), (0, pad)))
    wuq_r = jnp.pad(rot_half(w_uq[..., d_nope:]), ((0, 0), (0, 0), (d_nope, pad)))
    wuq = jnp.concatenate([wuq_a.reshape(q_lora, nh * SLAB), wuq_r.reshape(q_lora, nh * SLAB)], axis=1)

    sel = jnp.zeros((rope, nh, SLAB), F32).at[:, :, d_nope:dh].set(
        jnp.broadcast_to(jnp.eye(rope, dtype=F32)[:, None, :], (rope, nh, rope)))
    sel = jnp.pad(sel.reshape(rope, nh * SLAB), ((0, LANE - rope), (0, 0)))
    wuk = jnp.concatenate([jnp.pad(w_uk, ((0, 0), (0, 0), (0, SLAB - d_nope))).reshape(kvl, nh * SLAB),
                           sel, sel], axis=0)
    wukt = jnp.pad(jnp.transpose(w_uk, (1, 2, 0)), ((0, 0), (0, SLAB - d_nope), (0, 0)))

    scale = dh ** -0.5
    return dict(
        g_mix=g_norm_mix.reshape(1, d), wq=w_in[:, :c0].astype(BF16), g_q_lat=g_q_lat.reshape(1, q_lora),
        wuq=wuq.astype(BF16), gqh=(jnp.pad(g_q_head, (0, pad)) * scale).reshape(1, SLAB),
        wkv=wkv.astype(BF16), g_kv_lat=g_kv_lat.reshape(1, kvl), wuk=wuk.astype(BF16),
        gkh=jnp.pad(g_k_head, (0, pad)).reshape(1, SLAB), wuv=w_uv.reshape(kvl, -1).astype(BF16),
        wukt=wukt.astype(BF16),
        wukt_flat=jnp.transpose(w_uk, (1, 2, 0)).reshape(nh * d_nope, kvl).astype(BF16),
        wglu=w_in[:, c2:c3].astype(BF16), wgate=w_in[:, c3:].astype(BF16),
        wdw=jnp.pad(w_dw, ((0, CONV_PAD - conv_w), (0, 0))), b_dw=b_dw.reshape(1, cch),
        g_cn=g_conv_norm.reshape(1, cch), b_cn=b_conv_norm.reshape(1, cch))


def kernel(x_prompt, x_sample, c_prompt, c_sample, cache_kv_latent, cache_k_rope, state_conv, page_table, w_ada, b_ada, g_norm_ffn1, w_ffn1_in, w_ffn1_out, g_norm_mix, w_in, g_q_lat, w_uq, g_q_head, g_kv_lat, w_uk, w_uv, g_k_head, w_attn_out, w_dw, b_dw, g_conv_norm, b_conv_norm, w_conv_out, w_out, g_norm_ffn2, w_ffn2_in, w_ffn2_out):
    bp, t, d = x_prompt.shape
    bs, ts, _ = x_sample.shape
    assert ts == 1, "sample group handles one new token per sequence"
    q_lora, nh, dh = w_uq.shape
    kvl, _, d_nope = w_uk.shape
    d_v = w_uv.shape[2]
    rope = dh - d_nope
    conv_w, cch = w_dw.shape
    n_pages = page_table.shape[1]
    page = cache_kv_latent.shape[1]
    past_len = n_pages * page
    dims = (nh, dh, kvl, rope, conv_w, cch)

    tm_ffn = min(512, t)
    tm_mix = min(256, t)
    tm_merge = min(512, t)
    tq = min(256, t)
    tm_s = 32
    pages_per_step = min(16, n_pages)
    pos_tile = 256

    wts = _prep_weights(w_in, g_norm_mix, g_q_lat, w_uq, g_q_head, g_kv_lat, w_uk, w_uv, g_k_head,
                        w_dw, b_dw, g_conv_norm, b_conv_norm)
    w1i, w1o = w_ffn1_in.astype(BF16), w_ffn1_out.astype(BF16)
    w2i, w2o = w_ffn2_in.astype(BF16), w_ffn2_out.astype(BF16)
    wao = w_attn_out.reshape(nh * d_v, d).astype(BF16)
    wco = w_conv_out.astype(BF16)
    wo = w_out.astype(BF16)
    g1, g2 = g_norm_ffn1.reshape(1, d), g_norm_ffn2.reshape(1, d)

    mod = _ada(jnp.concatenate([c_prompt, c_sample], axis=0), w_ada, b_ada)
    n_mod = mod.shape[1] // d
    mod_p = [mod[:bp, i * d:(i + 1) * d].reshape(bp, 1, d) for i in range(n_mod)]
    mod_s = [mod[bp:, i * d:(i + 1) * d] for i in range(n_mod)]
    mod_s3 = [m_.reshape(1, bs, d) for m_ in mod_s]

    half = rope // 2
    inv = ROPE_THETA ** (-jnp.arange(half, dtype=F32) / half)
    inv_q = jnp.zeros((LANE,), F32).at[d_nope:dh].set(jnp.tile(inv, 2))
    inv_k = jnp.zeros((LANE,), F32).at[:rope].set(jnp.tile(inv, 2))
    inv2 = jnp.stack([inv_q, inv_k])
    tabs_p = _rope_tables(inv2, t, 0)
    tabs_s = _rope_tables(inv2, 8, past_len)

    xp = x_prompt.reshape(bp * t, d)
    hp = _ffn(xp, mod_p[0], mod_p[1], mod_p[2], g1, w1i, w1o, tm_ffn, t // tm_ffn)
    kvlat_p, krope_p, qx, kx, v, ycv_p, cst = _mixer_prompt(
        hp.reshape(bp, t, d), mod_p[3], mod_p[4], wts, tabs_p, dims, tm_mix)
    o_p = _attn_prompt(qx, kx, v, nh, d_v, tq)
    h2p = _merge(hp, mod_p[3], mod_p[4], mod_p[5], wts["g_mix"], o_p.reshape(bp * t, nh * d_v),
                 ycv_p.reshape(bp * t, cch), wts["wgate"], wao, wco, wo, tm_merge, t // tm_merge)
    y_p = _ffn(h2p, mod_p[6], mod_p[7], mod_p[8], g2, w2i, w2o, tm_ffn, t // tm_ffn)
    conv_state_p = cst[:, CONV_PAD - (conv_w - 1):, :]

    xs = x_sample.reshape(bs, d)
    hs = _ffn(xs, mod_s3[0], mod_s3[1], mod_s3[2], g1, w1i, w1o, bs, 1)
    kvlat_s, krope_s, qx_s, qg_s, qt_s, knew_s, z_s, ycv_s = _mixer_sample(
        hs, mod_s[3], mod_s[4], wts, tabs_s, state_conv, dims, tm_s)
    olat = _paged_attn(page_table, qg_s.reshape(bs, nh, SLAB), qt_s.reshape(bs, nh, kvl),
                       qx_s.reshape(bs, nh, SLAB), knew_s.reshape(bs, nh, SLAB),
                       kvlat_s.reshape(bs, 1, kvl), wts["wukt_flat"], cache_kv_latent, cache_k_rope,
                       (nh, dh, kvl, rope), pages_per_step, pos_tile)
    wuv_pair = w_uv.reshape(kvl, nh // 2, 2, d_v)
    zero = jnp.zeros((kvl, nh // 2, d_v), F32)
    wuv_a = jnp.transpose(jnp.concatenate([wuv_pair[:, :, 0], zero], axis=-1), (1, 0, 2)).astype(BF16)
    wuv_b = jnp.transpose(jnp.concatenate([zero, wuv_pair[:, :, 1]], axis=-1), (1, 0, 2)).astype(BF16)
    o_s = _uv_up(jnp.transpose(olat, (1, 0, 2)), wuv_a, wuv_b)
    h2s = _merge(hs, mod_s3[3], mod_s3[4], mod_s3[5], wts["g_mix"], o_s, ycv_s,
                 wts["wgate"], wao, wco, wo, bs, 1)
    y_s = _ffn(h2s, mod_s3[6], mod_s3[7], mod_s3[8], g2, w2i, w2o, bs, 1)
    conv_state_s = jnp.concatenate([state_conv[:, 1:, :], z_s[:, None, :]], axis=1)

    return (y_p.reshape(bp, t, d), y_s.reshape(bs, 1, d), kvlat_p, krope_p, conv_state_p,
            kvlat_s.reshape(bs, 1, kvl), krope_s.reshape(bs, 1, rope), conv_state_s)
```

```python
import functools

import jax
import jax.numpy as jnp
from jax import lax
from jax.experimental import pallas as pl
from jax.experimental.pallas import tpu as pltpu

F32 = jnp.float32
BF16 = jnp.bfloat16
EPS = 1e-6
ROPE_THETA = 10000.0
LOG2E = 1.4426950408889634
LANE = 128
SUBLANE = 8
SLAB = 128
CONV_PAD = 32
VMEM_LIMIT_BYTES = 56 * 1024 * 1024


def _cparams(*sem):
    return pltpu.CompilerParams(dimension_semantics=sem, vmem_limit_bytes=VMEM_LIMIT_BYTES)


def _dot(a, b):
    return jnp.dot(a, b, preferred_element_type=F32)


def _dot_nt(a, b):
    return lax.dot_general(a, b, (((1,), (1,)), ((), ())), preferred_element_type=F32)


def _rms(x, g):
    return x * lax.rsqrt(jnp.mean(x * x, axis=-1, keepdims=True) + EPS) * g


def _silu(x):
    return x * jax.nn.sigmoid(x)


def _hilo(x):
    hi = x.astype(BF16)
    lo = (x - hi.astype(F32)).astype(BF16)
    return hi, lo


def _const_spec(shape):
    nd = len(shape)
    return pl.BlockSpec(shape, lambda *_: (0,) * nd, pipeline_mode=pl.Buffered(1))


def _ada_kernel(c_ref, w_ref, b_ref, o_ref):
    a = _silu(c_ref[...]).astype(BF16)
    o_ref[...] = _dot(a, w_ref[...].astype(BF16)) + b_ref[...]


def _ada(c_all, w_ada, b_ada):
    m, d = c_all.shape
    n = w_ada.shape[1]
    tn = n // 8
    return pl.pallas_call(
        _ada_kernel,
        grid=(n // tn,),
        in_specs=[pl.BlockSpec((m, d), lambda j: (0, 0)),
                  pl.BlockSpec((d, tn), lambda j: (0, j)),
                  pl.BlockSpec((1, tn), lambda j: (0, j))],
        out_specs=pl.BlockSpec((m, tn), lambda j: (0, j)),
        out_shape=jax.ShapeDtypeStruct((m, n), F32),
        compiler_params=_cparams("arbitrary"),
        name="ada",
    )(c_all, w_ada, b_ada.reshape(1, n))


def _rope_kernel(inv_ref, cq_ref, sq_ref, ck_ref, sk_ref, *, pos0):
    t = cq_ref.shape[0]
    pos = (lax.broadcasted_iota(jnp.int32, (t, LANE), 0) + pos0).astype(F32)
    aq = pos * inv_ref[0:1, :]
    ak = pos * inv_ref[1:2, :]
    cq_ref[...] = jnp.cos(aq)
    sq_ref[...] = jnp.sin(aq)
    ck_ref[...] = jnp.cos(ak)
    sk_ref[...] = jnp.sin(ak)


def _rope_tables(inv2, t, pos0):
    shp = jax.ShapeDtypeStruct((t, LANE), F32)
    return pl.pallas_call(
        functools.partial(_rope_kernel, pos0=pos0),
        out_shape=(shp, shp, shp, shp),
        name="rope_tables",
    )(inv2)


def _ffn_kernel(x_ref, sh_ref, sc_ref, gate_ref, g_ref, win_ref, wout_ref, o_ref, *, f, fc):
    x = x_ref[...]
    u = (_rms(x, g_ref[...]) * (1.0 + sc_ref[...]) + sh_ref[...]).astype(BF16)
    acc = None
    for c in range(f // fc):
        up = _dot(u, win_ref[:, c * fc:(c + 1) * fc])
        gt = _dot(u, win_ref[:, f + c * fc:f + (c + 1) * fc])
        a = (_silu(gt) * up).astype(BF16)
        part = _dot(a, wout_ref[c * fc:(c + 1) * fc, :])
        acc = part if acc is None else acc + part
    o_ref[...] = x + 0.5 * gate_ref[...] * acc


def _mod_spec(arr, tiles_per_seq):
    r, d = arr.shape[1:]
    return pl.BlockSpec((None, r, d), lambda i: (i // tiles_per_seq, 0, 0))


def _ffn(x, shift, scale, gate, g, w_in, w_out, tm, tiles_per_seq):
    m, d = x.shape
    f = w_out.shape[0]
    fc = f // 2
    return pl.pallas_call(
        functools.partial(_ffn_kernel, f=f, fc=fc),
        grid=(m // tm,),
        in_specs=[pl.BlockSpec((tm, d), lambda i: (i, 0)),
                  _mod_spec(shift, tiles_per_seq), _mod_spec(scale, tiles_per_seq),
                  _mod_spec(gate, tiles_per_seq),
                  _const_spec((1, d)), _const_spec(w_in.shape), _const_spec(w_out.shape)],
        out_specs=pl.BlockSpec((tm, d), lambda i: (i, 0)),
        out_shape=jax.ShapeDtypeStruct((m, d), F32),
        compiler_params=_cparams("arbitrary"),
        name="ffn",
    )(x, shift, scale, gate, g, w_in, w_out)


def _head_norm_store(x, gain, o_ref, n_heads, d_head):
    for h in range(n_heads):
        xh = x[:, h * SLAB:(h + 1) * SLAB]
        ss = jnp.sum(xh * xh, axis=-1, keepdims=True)
        rinv = lax.rsqrt(ss * (1.0 / d_head) + EPS)
        o_ref[:, h * SLAB:(h + 1) * SLAB] = (xh * rinv * gain).astype(o_ref.dtype)


def _q_heads(qn, wuq_ref, cq, sq, gain, o_ref, n_heads, d_head):
    hw = n_heads * SLAB
    group = 4
    for hc in range(n_heads // group):
        lo_, hi_ = hc * group * SLAB, (hc + 1) * group * SLAB
        a = _dot(qn, wuq_ref[:, lo_:hi_])
        r = _dot(qn, wuq_ref[:, hw + lo_:hw + hi_])
        for j in range(group):
            h = hc * group + j
            qh = a[:, j * SLAB:(j + 1) * SLAB] * cq + r[:, j * SLAB:(j + 1) * SLAB] * sq
            ss = jnp.sum(qh * qh, axis=-1, keepdims=True)
            rinv = lax.rsqrt(ss * (1.0 / d_head) + EPS)
            o_ref[:, h * SLAB:(h + 1) * SLAB] = (qh * rinv * gain).astype(o_ref.dtype)


def _latent_paths(u, wkv_ref, gkv_ref, ck, sk, kv_lora, rope):
    kvpe = _dot(u, wkv_ref[...])
    kv_lat = _rms(kvpe[:, :kv_lora], gkv_ref[...])
    kr = kvpe[:, kv_lora:kv_lora + LANE] * ck + kvpe[:, kv_lora + LANE:kv_lora + 2 * LANE] * sk
    return kv_lat, kr


def _key_ext(kv_lat_bf, kr, wuk_ref):
    kr_hi, kr_lo = _hilo(kr)
    x = jnp.concatenate([kv_lat_bf, kr_hi, kr_lo], axis=-1)
    return _dot(x, wuk_ref[...])


def _causal_dwconv(zb_ref, wdw_ref, bdw_ref, y_ref, tm, conv_w, conv_ch, cw=256):
    off = CONV_PAD - (conv_w - 1)
    nblk = (off + conv_w - 1) // SUBLANE + 1
    assert off > 0 and SUBLANE * (nblk - 1) + 1 - off >= conv_w
    rowi = lax.broadcasted_iota(jnp.int32, (SUBLANE, cw), 0)

    for cc in range(conv_ch // cw):
        cs = slice(cc * cw, (cc + 1) * cw)
        wrow = [wdw_ref[k:k + 1, cs] for k in range(conv_w)]
        bias = bdw_ref[:, cs]

        def phase_sum(s, blocks, first):
            acc = None
            for j in range(nblk):
                k = SUBLANE * j + s - off
                if 0 <= k < conv_w:
                    term = blocks[j - first] * wrow[k]
                    acc = term if acc is None else acc + term
            return acc

        def rolled(s, blocks):
            return pltpu.roll(phase_sum(s, blocks, 0), SUBLANE - s, 0)

        def load_blocks(first_row):
            return [zb_ref[pl.ds(first_row + SUBLANE * j, SUBLANE), cs] for j in range(nblk - 1)]

        carry0 = tuple(rolled(s, load_blocks(0)) for s in range(1, SUBLANE))

        def body(r, carry):
            base = pl.multiple_of(r * SUBLANE, SUBLANE)
            blk = load_blocks(base + SUBLANE)
            y = phase_sum(0, blk, 1) + bias
            new = tuple(rolled(s, blk) for s in range(1, SUBLANE))
            for s in range(1, SUBLANE):
                y = y + jnp.where(rowi < SUBLANE - s, carry[s - 1], new[s - 1])
            y_ref[pl.ds(base, SUBLANE), cs] = y
            return new

        lax.fori_loop(0, tm // SUBLANE, body, carry0)


def _mixp_kernel(h_ref, sh_ref, sc_ref, gmix_ref, wq_ref, gql_ref, wuq_ref, gqh_ref,
                 wkv_ref, gkv_ref, wuk_ref, gkh_ref, wuvt_ref, wglu_ref,
                 cq_ref, sq_ref, ck_ref, sk_ref, wdw_ref, bdw_ref, gcn_ref, bcn_ref,
                 kvlat_ref, krope_ref, qx_ref, kx_ref, vt_ref, ycv_ref, cst_ref,
                 zb_ref, y_ref, *, n_heads, d_head, kv_lora, rope, conv_w, conv_ch):
    t = pl.program_id(1)
    nt = pl.num_programs(1)
    tm = h_ref.shape[0]
    u = (_rms(h_ref[...], gmix_ref[...]) * (1.0 + sc_ref[...]) + sh_ref[...]).astype(BF16)

    qn = _rms(_dot(u, wq_ref[...]), gql_ref[...]).astype(BF16)
    _q_heads(qn, wuq_ref, cq_ref[...], sq_ref[...], gqh_ref[...], qx_ref, n_heads, d_head)

    kv_lat, kr = _latent_paths(u, wkv_ref, gkv_ref, ck_ref[...], sk_ref[...], kv_lora, rope)
    kvlat_ref[...] = kv_lat
    krope_ref[...] = kr[:, :rope]
    kv_bf = kv_lat.astype(BF16)
    _head_norm_store(_key_ext(kv_bf, kr, wuk_ref), gkh_ref[...], kx_ref, n_heads, d_head)
    vt_ref[...] = _dot_nt(wuvt_ref[...], kv_bf).astype(BF16)

    glu = _dot(u, wglu_ref[...])
    z = glu[:, :conv_ch] * jax.nn.sigmoid(glu[:, conv_ch:])

    @pl.when(t == 0)
    def _():
        zb_ref[0:CONV_PAD, :] = jnp.zeros((CONV_PAD, conv_ch), F32)

    zb_ref[CONV_PAD:CONV_PAD + tm, :] = z
    _causal_dwconv(zb_ref, wdw_ref, bdw_ref, y_ref, tm, conv_w, conv_ch)

    y = y_ref[...]
    mu = jnp.mean(y, axis=-1, keepdims=True)
    yc = y - mu
    var = jnp.mean(yc * yc, axis=-1, keepdims=True)
    yn = yc * lax.rsqrt(var + EPS) * gcn_ref[...] + bcn_ref[...]
    ycv_ref[...] = _silu(yn).astype(BF16)

    tail = zb_ref[tm:tm + CONV_PAD, :]
    zb_ref[0:CONV_PAD, :] = tail

    @pl.when(t == nt - 1)
    def _():
        cst_ref[...] = tail


def _mixer_prompt(h, shift, scale, wts, tabs, dims, tm):
    b, t, d = h.shape
    nh, dh, kvl, rope, cw_, cch = dims
    nt = t // tm
    hw = nh * SLAB
    kern = functools.partial(_mixp_kernel, n_heads=nh, d_head=dh, kv_lora=kvl, rope=rope,
                             conv_w=cw_, conv_ch=cch)
    tok = lambda n: pl.BlockSpec((None, tm, n), lambda i, j: (i, j, 0))
    mod = pl.BlockSpec((None, 1, d), lambda i, j: (i, 0, 0))
    tab = pl.BlockSpec((tm, LANE), lambda i, j: (j, 0))
    wnames = ("g_mix", "wq", "g_q_lat", "wuq", "gqh_exp2", "wkv", "g_kv_lat", "wuk", "gkh", "wuvt", "wglu")
    cnames = ("wdw", "b_dw", "g_cn", "b_cn")
    vdim = wts["wuvt"].shape[0]
    in_specs = ([tok(d), mod, mod] + [_const_spec(wts[n].shape) for n in wnames] + [tab] * 4
                + [_const_spec(wts[n].shape) for n in cnames])
    out_shape = (jax.ShapeDtypeStruct((b, t, kvl), F32), jax.ShapeDtypeStruct((b, t, rope), F32),
                 jax.ShapeDtypeStruct((b, t, hw), BF16), jax.ShapeDtypeStruct((b, t, hw), BF16),
                 jax.ShapeDtypeStruct((b, vdim, t), BF16), jax.ShapeDtypeStruct((b, t, cch), BF16),
                 jax.ShapeDtypeStruct((b, CONV_PAD, cch), F32))
    out_specs = (tok(kvl), tok(rope), tok(hw), tok(hw),
                 pl.BlockSpec((None, vdim, tm), lambda i, j: (i, 0, j)), tok(cch),
                 pl.BlockSpec((None, CONV_PAD, cch), lambda i, j: (i, 0, 0)))
    return pl.pallas_call(
        kern, grid=(b, nt), in_specs=in_specs, out_specs=out_specs, out_shape=out_shape,
        scratch_shapes=[pltpu.VMEM((tm + CONV_PAD, cch), F32), pltpu.VMEM((tm, cch), F32)],
        compiler_params=_cparams("arbitrary", "arbitrary"),
        name="mixer_prompt",
    )(h, shift, scale, *[wts[n] for n in wnames], *tabs, *[wts[n] for n in cnames])


def _attn_kernel(q_ref, k_ref, vt_ref, o_ref, *, tq, d_v):
    t = q_ref.shape[0]
    kv_i = lax.broadcasted_iota(jnp.int32, (tq, tq), 0)
    q_i = lax.broadcasted_iota(jnp.int32, (tq, tq), 1)
    causal = kv_i <= q_i
    ones = jnp.ones((2 * SUBLANE, t), BF16)
    for qi in range(t // tq):
        rows = slice(qi * tq, (qi + 1) * tq)
        n_kv = (qi + 1) * tq
        vt = jnp.concatenate([vt_ref[:, 0:n_kv], ones[:, 0:n_kv]], axis=0)
        halves = []
        for hh in range(2):
            lanes = slice(hh * SLAB, (hh + 1) * SLAB)
            q = q_ref[rows, lanes]
            sd = jnp.where(causal, _dot_nt(k_ref[rows, lanes], q), -jnp.inf)
            m = jnp.max(sd, axis=0, keepdims=True)
            if qi > 0:
                so = _dot_nt(k_ref[0:qi * tq, lanes], q)
                m = jnp.maximum(m, jnp.max(so, axis=0, keepdims=True))
                p = jnp.concatenate([jnp.exp2(so - m).astype(BF16), jnp.exp2(sd - m).astype(BF16)], axis=0)
            else:
                p = jnp.exp2(sd - m).astype(BF16)
            ot = _dot(vt, p)
            halves.append(ot[hh * d_v:(hh + 1) * d_v, :] / ot[2 * d_v:2 * d_v + 1, :])
        o_ref[rows, :] = jnp.concatenate(halves, axis=0).T.astype(o_ref.dtype)


def _attn_prompt(qx, kx, vt, n_heads, d_v, tq):
    b, t, _ = qx.shape
    pairs = n_heads // 2
    return pl.pallas_call(
        functools.partial(_attn_kernel, tq=tq, d_v=d_v),
        grid=(b, pairs),
        in_specs=[pl.BlockSpec((None, t, 2 * SLAB), lambda i, p: (i, 0, p)),
                  pl.BlockSpec((None, t, 2 * SLAB), lambda i, p: (i, 0, p)),
                  pl.BlockSpec((None, 2 * d_v, t), lambda i, p: (i, p, 0))],
        out_specs=pl.BlockSpec((None, t, 2 * d_v), lambda i, p: (i, 0, p)),
        out_shape=jax.ShapeDtypeStruct((b, t, n_heads * d_v), BF16),
        compiler_params=_cparams("arbitrary", "arbitrary"),
        name="attn_prompt",
    )(qx, kx, vt)


def _merge_kernel(h_ref, sh_ref, sc_ref, gate_ref, g_ref, o_ref, y_ref,
                  wg_ref, wao_ref, wco_ref, wout_ref, out_ref):
    h = h_ref[...]
    u = (_rms(h, g_ref[...]) * (1.0 + sc_ref[...]) + sh_ref[...]).astype(BF16)
    gl = _dot(u, wg_ref[...])
    d = wao_ref.shape[1]
    a = _dot(o_ref[...], wao_ref[...])
    b = _dot(y_ref[...], wco_ref[...])
    merged = jax.nn.sigmoid(gl[:, :d]) * a + jax.nn.sigmoid(gl[:, d:]) * b
    out_ref[...] = h + gate_ref[...] * _dot(merged.astype(BF16), wout_ref[...])


def _merge(h, shift, scale, gate, g, o, ycv, wg, wao, wco, wout, tm, tiles_per_seq):
    m, d = h.shape
    row = lambda n: pl.BlockSpec((tm, n), lambda i: (i, 0))
    return pl.pallas_call(
        _merge_kernel,
        grid=(m // tm,),
        in_specs=[row(d), _mod_spec(shift, tiles_per_seq), _mod_spec(scale, tiles_per_seq),
                  _mod_spec(gate, tiles_per_seq), _const_spec((1, d)), row(o.shape[1]), row(ycv.shape[1]),
                  _const_spec(wg.shape), _const_spec(wao.shape), _const_spec(wco.shape),
                  _const_spec(wout.shape)],
        out_specs=row(d),
        out_shape=jax.ShapeDtypeStruct((m, d), F32),
        compiler_params=_cparams("arbitrary"),
        name="merge",
    )(h, shift, scale, gate, g, o, ycv, wg, wao, wco, wout)


def _mixs_kernel(h_ref, sh_ref, sc_ref, gmix_ref, wq_ref, gql_ref, wuq_ref, gqh_ref,
                 wkv_ref, gkv_ref, wuk_ref, gkh_ref, wukt_ref, wglu_ref,
                 cq_ref, sq_ref, ck_ref, sk_ref, wdw_ref, bdw_ref, gcn_ref, bcn_ref, st_ref,
                 kvlat_ref, krope_ref, qx_ref, qg_ref, qt_ref, knew_ref, z_ref, ycv_ref,
                 *, n_heads, d_head, kv_lora, rope, conv_w, conv_ch):
    u = (_rms(h_ref[...], gmix_ref[...]) * (1.0 + sc_ref[...]) + sh_ref[...]).astype(BF16)

    qn = _rms(_dot(u, wq_ref[...]), gql_ref[...]).astype(BF16)
    _q_heads(qn, wuq_ref, cq_ref[0:1, :], sq_ref[0:1, :], gqh_ref[...], qx_ref, n_heads, d_head)
    for h in range(n_heads):
        qg = qx_ref[:, h * SLAB:(h + 1) * SLAB] * gkh_ref[...]
        qg_ref[:, h * SLAB:(h + 1) * SLAB] = qg
        qt_ref[:, h * kv_lora:(h + 1) * kv_lora] = _dot(qg.astype(BF16), wukt_ref[h])

    kv_lat, kr = _latent_paths(u, wkv_ref, gkv_ref, ck_ref[0:1, :], sk_ref[0:1, :], kv_lora, rope)
    kvlat_ref[...] = kv_lat
    krope_ref[...] = kr[:, :rope]
    _head_norm_store(_key_ext(kv_lat.astype(BF16), kr, wuk_ref), gkh_ref[...], knew_ref, n_heads, d_head)

    glu = _dot(u, wglu_ref[...])
    z = glu[:, :conv_ch] * jax.nn.sigmoid(glu[:, conv_ch:])
    z_ref[...] = z
    y = z * wdw_ref[conv_w - 1:conv_w, :] + bdw_ref[...]
    for k in range(conv_w - 1):
        y = y + st_ref[k] * wdw_ref[k:k + 1, :]
    mu = jnp.mean(y, axis=-1, keepdims=True)
    yc = y - mu
    var = jnp.mean(yc * yc, axis=-1, keepdims=True)
    yn = yc * lax.rsqrt(var + EPS) * gcn_ref[...] + bcn_ref[...]
    ycv_ref[...] = _silu(yn).astype(BF16)


def _mixer_sample(h, shift, scale, wts, tabs, state, dims, tm):
    m, d = h.shape
    nh, dh, kvl, rope, cw_, cch = dims
    hw = nh * SLAB
    kern = functools.partial(_mixs_kernel, n_heads=nh, d_head=dh, kv_lora=kvl, rope=rope,
                             conv_w=cw_, conv_ch=cch)
    row = lambda n: pl.BlockSpec((tm, n), lambda i: (i, 0))
    tab = pl.BlockSpec((8, LANE), lambda i: (0, 0))
    wnames = ("g_mix", "wq", "g_q_lat", "wuq", "gqh", "wkv", "g_kv_lat", "wuk", "gkh", "wukt", "wglu")
    cnames = ("wdw", "b_dw", "g_cn", "b_cn")
    in_specs = ([row(d), row(d), row(d)] + [_const_spec(wts[n].shape) for n in wnames] + [tab] * 4
                + [_const_spec(wts[n].shape) for n in cnames]
                + [pl.BlockSpec((cw_ - 1, tm, cch), lambda i: (0, i, 0))])
    out_shape = (jax.ShapeDtypeStruct((m, kvl), F32), jax.ShapeDtypeStruct((m, rope), F32),
                 jax.ShapeDtypeStruct((m, hw), F32), jax.ShapeDtypeStruct((m, hw), F32),
                 jax.ShapeDtypeStruct((m, nh * kvl), F32), jax.ShapeDtypeStruct((m, hw), F32),
                 jax.ShapeDtypeStruct((m, cch), F32), jax.ShapeDtypeStruct((m, cch), BF16))
    out_specs = (row(kvl), row(rope), row(hw), row(hw), row(nh * kvl), row(hw), row(cch), row(cch))
    return pl.pallas_call(
        kern, grid=(m // tm,), in_specs=in_specs, out_specs=out_specs, out_shape=out_shape,
        compiler_params=_cparams("arbitrary"),
        name="mixer_sample",
    )(h, shift, scale, *[wts[n] for n in wnames], *tabs, *[wts[n] for n in cnames], state)


def _paged_kernel(pt_ref, qg_ref, qt_ref, qx_ref, knew_ref, cnew_ref, wukt_ref, kv_hbm, kr_hbm,
                  o_ref, kvbuf, krbuf, sem, cbf, s_scr, m_scr, l_scr, acc_scr,
                  *, pages, page, tile, n_heads, d_head, d_nope, rope):
    b = pl.program_id(0)
    c = pl.program_id(1)
    nc = pl.num_programs(1)
    step = b * nc + c
    nsteps = pl.num_programs(0) * nc
    slot = step % 2
    npos = pages * page

    def copies(step_i, slot_i):
        out = []
        for p in range(pages):
            pg = pt_ref[step_i * pages + p]
            out.append(pltpu.make_async_copy(
                kv_hbm.at[pg], kvbuf.at[slot_i, pl.ds(p * page, page), :], sem.at[0, slot_i]))
            out.append(pltpu.make_async_copy(
                kr_hbm.at[pg], krbuf.at[slot_i, :, pl.ds(p * page, page)], sem.at[1, slot_i]))
        return out

    @pl.when(step == 0)
    def _():
        for cp in copies(0, 0):
            cp.start()

    nxt = jnp.minimum(step + 1, nsteps - 1)
    for cp in copies(nxt, 1 - slot):
        cp.start()

    for cp in copies(step, slot):
        cp.wait()

    @pl.when(step == nsteps - 1)
    def _():
        for cp in copies(nxt, 1 - slot):
            cp.wait()

    @pl.when(c == 0)
    def _():
        m_scr[...] = jnp.full(m_scr.shape, -jnp.inf, F32)
        l_scr[...] = jnp.zeros(l_scr.shape, F32)
        acc_scr[...] = jnp.zeros(acc_scr.shape, F32)

    qt_hi, qt_lo = _hilo(qt_ref[...])
    qtl = jnp.concatenate([qt_hi, qt_lo], axis=0)
    qg_hi, qg_lo = _hilo(qg_ref[:, d_nope:d_head])
    qgl = jnp.concatenate([qg_hi, qg_lo], axis=0)

    krt = krbuf[slot]
    krss = jnp.sum(krt * krt, axis=0, keepdims=True)
    sr = _dot(qgl, krt.astype(BF16))
    s_rope = sr[:n_heads] + sr[n_heads:]

    for j in range(npos // tile):
        cols = slice(j * tile, (j + 1) * tile)
        cb = kvbuf[slot, cols, :].astype(BF16)
        cbf[cols, :] = cb
        kn = _dot_nt(wukt_ref[...], cb)
        ssn = jnp.sum((kn * kn).reshape(n_heads, d_nope, tile), axis=1)
        sn = _dot_nt(qtl, cb)
        rinv = lax.rsqrt((ssn + krss[:, cols]) * (1.0 / d_head) + EPS)
        s_scr[:, cols] = (sn[:n_heads] + sn[n_heads:] + s_rope[:, cols]) * rinv

    s = s_scr[...]
    m_prev = m_scr[...]
    m_new = jnp.maximum(m_prev, jnp.max(s, axis=-1, keepdims=True))
    corr = jnp.exp(m_prev - m_new)
    p = jnp.exp(s - m_new)
    l_new = l_scr[...] * corr + jnp.sum(p, axis=-1, keepdims=True)
    acc_new = acc_scr[...] * corr + _dot(p.astype(BF16), cbf[...])
    m_scr[...] = m_new
    l_scr[...] = l_new
    acc_scr[...] = acc_new

    @pl.when(c == nc - 1)
    def _():
        s_self = jnp.sum(qx_ref[...] * knew_ref[...], axis=-1, keepdims=True)
        m2 = jnp.maximum(m_new, s_self)
        corr2 = jnp.exp(m_new - m2)
        p_self = jnp.exp(s_self - m2)
        l2 = l_new * corr2 + p_self
        o_ref[...] = (acc_new * corr2 + p_self * cnew_ref[...]) / l2


def _paged_attn(page_table, qg, qt, qx, knew, cnew, wukt, cache_kv, cache_kr, dims, pages, tile):
    nb, n_pages = page_table.shape
    nh, dh, kvl, rope = dims
    d_nope = dh - rope
    page = cache_kv.shape[1]
    nc = n_pages // pages
    npos = pages * page
    kern = functools.partial(_paged_kernel, pages=pages, page=page, tile=tile, n_heads=nh,
                             d_head=dh, d_nope=d_nope, rope=rope)
    per_b = lambda r, n: pl.BlockSpec((None, r, n), lambda i, j, pt: (i, 0, 0))
    grid_spec = pltpu.PrefetchScalarGridSpec(
        num_scalar_prefetch=1,
        grid=(nb, nc),
        in_specs=[per_b(nh, SLAB), per_b(nh, kvl), per_b(nh, SLAB), per_b(nh, SLAB), per_b(1, kvl),
                  pl.BlockSpec(wukt.shape, lambda i, j, pt: (0, 0)),
                  pl.BlockSpec(memory_space=pl.ANY), pl.BlockSpec(memory_space=pl.ANY)],
        out_specs=per_b(nh, kvl),
        scratch_shapes=[pltpu.VMEM((2, npos, kvl), F32), pltpu.VMEM((2, rope, npos), F32),
                        pltpu.SemaphoreType.DMA((2, 2)),
                        pltpu.VMEM((npos, kvl), BF16), pltpu.VMEM((nh, npos), F32),
                        pltpu.VMEM((nh, 1), F32), pltpu.VMEM((nh, 1), F32), pltpu.VMEM((nh, kvl), F32)])
    return pl.pallas_call(
        kern, grid_spec=grid_spec,
        out_shape=jax.ShapeDtypeStruct((nb, nh, kvl), F32),
        compiler_params=_cparams("arbitrary", "arbitrary"),
        name="paged_attn",
    )(page_table.reshape(-1), qg, qt, qx, knew, cnew, wukt, cache_kv, cache_kr)


def _uvup_kernel(ol_ref, wa_ref, wb_ref, o_ref):
    for p in range(wa_ref.shape[0]):
        a = _dot(ol_ref[2 * p].astype(BF16), wa_ref[p])
        b = _dot(ol_ref[2 * p + 1].astype(BF16), wb_ref[p])
        o_ref[:, p * LANE:(p + 1) * LANE] = (a + b).astype(o_ref.dtype)


def _uv_up(olat_t, wa, wb):
    nh, m, _ = olat_t.shape
    return pl.pallas_call(
        _uvup_kernel,
        out_shape=jax.ShapeDtypeStruct((m, wa.shape[0] * LANE), BF16),
        compiler_params=pltpu.CompilerParams(vmem_limit_bytes=VMEM_LIMIT_BYTES),
        name="uv_up",
    )(olat_t, wa, wb)


def _prep_weights(w_in, g_norm_mix, g_q_lat, w_uq, g_q_head, g_kv_lat, w_uk, w_uv, g_k_head,
                  w_dw, b_dw, g_conv_norm, b_conv_norm):
    d = w_in.shape[0]
    q_lora, nh, dh = w_uq.shape
    kvl, _, d_nope = w_uk.shape
    rope = dh - d_nope
    half = rope // 2
    conv_w, cch = w_dw.shape
    pad = SLAB - dh
    c0, c1, c2, c3 = q_lora, q_lora + kvl, q_lora + kvl + rope, q_lora + kvl + rope + 2 * cch

    def rot_half(x):
        return jnp.concatenate([-x[..., half:], x[..., :half]], axis=-1)

    pe = w_in[:, c1:c2]
    zpad = jnp.zeros((d, LANE - rope), F32)
    wkv = jnp.concatenate([w_in[:, c0:c1], pe, zpad, rot_half(pe), zpad], axis=1)

    wuq_a = jnp.pad(w_uq, ((0, 0), (0, 0), (0, pad)))
    wuq_r = jnp.pad(rot_half(w_uq[..., d_nope:]), ((0, 0), (0, 0), (d_nope, pad)))
    wuq = jnp.concatenate([wuq_a.reshape(q_lora, nh * SLAB), wuq_r.reshape(q_lora, nh * SLAB)], axis=1)

    sel = jnp.zeros((rope, nh, SLAB), F32).at[:, :, d_nope:dh].set(
        jnp.broadcast_to(jnp.eye(rope, dtype=F32)[:, None, :], (rope, nh, rope)))
    sel = jnp.pad(sel.reshape(rope, nh * SLAB), ((0, LANE - rope), (0, 0)))
    wuk = jnp.concatenate([jnp.pad(w_uk, ((0, 0), (0, 0), (0, SLAB - d_nope))).reshape(kvl, nh * SLAB),
                           sel, sel], axis=0)
    wukt = jnp.pad(jnp.transpose(w_uk, (1, 2, 0)), ((0, 0), (0, SLAB - d_nope), (0, 0)))

    scale = dh ** -0.5
    return dict(
        g_mix=g_norm_mix.reshape(1, d), wq=w_in[:, :c0].astype(BF16), g_q_lat=g_q_lat.reshape(1, q_lora),
        wuq=wuq.astype(BF16), gqh=(jnp.pad(g_q_head, (0, pad)) * scale).reshape(1, SLAB),
        wkv=wkv.astype(BF16), g_kv_lat=g_kv_lat.reshape(1, kvl), wuk=wuk.astype(BF16),
        gqh_exp2=(jnp.pad(g_q_head, (0, pad)) * (scale * LOG2E)).reshape(1, SLAB),
        gkh=jnp.pad(g_k_head, (0, pad)).reshape(1, SLAB),
        wuvt=jnp.transpose(w_uv.reshape(kvl, -1)).astype(BF16),
        wukt=wukt.astype(BF16),
        wukt_flat=jnp.transpose(w_uk, (1, 2, 0)).reshape(nh * d_nope, kvl).astype(BF16),
        wglu=w_in[:, c2:c3].astype(BF16), wgate=w_in[:, c3:].astype(BF16),
        wdw=jnp.pad(w_dw, ((0, CONV_PAD - conv_w), (0, 0))), b_dw=b_dw.reshape(1, cch),
        g_cn=g_conv_norm.reshape(1, cch), b_cn=b_conv_norm.reshape(1, cch))


def kernel(x_prompt, x_sample, c_prompt, c_sample, cache_kv_latent, cache_k_rope, state_conv, page_table, w_ada, b_ada, g_norm_ffn1, w_ffn1_in, w_ffn1_out, g_norm_mix, w_in, g_q_lat, w_uq, g_q_head, g_kv_lat, w_uk, w_uv, g_k_head, w_attn_out, w_dw, b_dw, g_conv_norm, b_conv_norm, w_conv_out, w_out, g_norm_ffn2, w_ffn2_in, w_ffn2_out):
    bp, t, d = x_prompt.shape
    bs, ts, _ = x_sample.shape
    assert ts == 1, "sample group handles one new token per sequence"
    q_lora, nh, dh = w_uq.shape
    kvl, _, d_nope = w_uk.shape
    d_v = w_uv.shape[2]
    rope = dh - d_nope
    conv_w, cch = w_dw.shape
    n_pages = page_table.shape[1]
    page = cache_kv_latent.shape[1]
    past_len = n_pages * page
    dims = (nh, dh, kvl, rope, conv_w, cch)

    tm_ffn = min(512, t)
    tm_mix = min(256, t)
    tm_merge = min(512, t)
    tq = min(512, t)
    tm_s = 32
    pages_per_step = min(16, n_pages)
    pos_tile = 256

    wts = _prep_weights(w_in, g_norm_mix, g_q_lat, w_uq, g_q_head, g_kv_lat, w_uk, w_uv, g_k_head,
                        w_dw, b_dw, g_conv_norm, b_conv_norm)
    w1i, w1o = w_ffn1_in.astype(BF16), w_ffn1_out.astype(BF16)
    w2i, w2o = w_ffn2_in.astype(BF16), w_ffn2_out.astype(BF16)
    wao = w_attn_out.reshape(nh * d_v, d).astype(BF16)
    wco = w_conv_out.astype(BF16)
    wo = w_out.astype(BF16)
    g1, g2 = g_norm_ffn1.reshape(1, d), g_norm_ffn2.reshape(1, d)

    mod = _ada(jnp.concatenate([c_prompt, c_sample], axis=0), w_ada, b_ada)
    n_mod = mod.shape[1] // d
    mod_p = [mod[:bp, i * d:(i + 1) * d].reshape(bp, 1, d) for i in range(n_mod)]
    mod_s = [mod[bp:, i * d:(i + 1) * d] for i in range(n_mod)]
    mod_s3 = [m_.reshape(1, bs, d) for m_ in mod_s]

    half = rope // 2
    inv = ROPE_THETA ** (-jnp.arange(half, dtype=F32) / half)
    inv_q = jnp.zeros((LANE,), F32).at[d_nope:dh].set(jnp.tile(inv, 2))
    inv_k = jnp.zeros((LANE,), F32).at[:rope].set(jnp.tile(inv, 2))
    inv2 = jnp.stack([inv_q, inv_k])
    tabs_p = _rope_tables(inv2, t, 0)
    tabs_s = _rope_tables(inv2, 8, past_len)

    xp = x_prompt.reshape(bp * t, d)
    hp = _ffn(xp, mod_p[0], mod_p[1], mod_p[2], g1, w1i, w1o, tm_ffn, t // tm_ffn)
    kvlat_p, krope_p, qx, kx, v, ycv_p, cst = _mixer_prompt(
        hp.reshape(bp, t, d), mod_p[3], mod_p[4], wts, tabs_p, dims, tm_mix)
    o_p = _attn_prompt(qx, kx, v, nh, d_v, tq)
    h2p = _merge(hp, mod_p[3], mod_p[4], mod_p[5], wts["g_mix"], o_p.reshape(bp * t, nh * d_v),
                 ycv_p.reshape(bp * t, cch), wts["wgate"], wao, wco, wo, tm_merge, t // tm_merge)
    y_p = _ffn(h2p, mod_p[6], mod_p[7], mod_p[8], g2, w2i, w2o, tm_ffn, t // tm_ffn)
    conv_state_p = cst[:, CONV_PAD - (conv_w - 1):, :]

    xs = x_sample.reshape(bs, d)
    hs = _ffn(xs, mod_s3[0], mod_s3[1], mod_s3[2], g1, w1i, w1o, bs, 1)
    kvlat_s, krope_s, qx_s, qg_s, qt_s, knew_s, z_s, ycv_s = _mixer_sample(
        hs, mod_s[3], mod_s[4], wts, tabs_s, jnp.transpose(state_conv, (1, 0, 2)), dims, tm_s)
    olat = _paged_attn(page_table, qg_s.reshape(bs, nh, SLAB), qt_s.reshape(bs, nh, kvl),
                       qx_s.reshape(bs, nh, SLAB), knew_s.reshape(bs, nh, SLAB),
                       kvlat_s.reshape(bs, 1, kvl), wts["wukt_flat"], cache_kv_latent,
                       jnp.swapaxes(cache_k_rope, 1, 2),
                       (nh, dh, kvl, rope), pages_per_step, pos_tile)
    wuv_pair = w_uv.reshape(kvl, nh // 2, 2, d_v)
    zero = jnp.zeros((kvl, nh // 2, d_v), F32)
    wuv_a = jnp.transpose(jnp.concatenate([wuv_pair[:, :, 0], zero], axis=-1), (1, 0, 2)).astype(BF16)
    wuv_b = jnp.transpose(jnp.concatenate([zero, wuv_pair[:, :, 1]], axis=-1), (1, 0, 2)).astype(BF16)
    o_s = _uv_up(jnp.transpose(olat, (1, 0, 2)), wuv_a, wuv_b)
    h2s = _merge(hs, mod_s3[3], mod_s3[4], mod_s3[5], wts["g_mix"], o_s, ycv_s,
                 wts["wgate"], wao, wco, wo, bs, 1)
    y_s = _ffn(h2s, mod_s3[6], mod_s3[7], mod_s3[8], g2, w2i, w2o, bs, 1)
    conv_state_s = jnp.concatenate([state_conv[:, 1:, :], z_s[:, None, :]], axis=1)

    return (y_p.reshape(bp, t, d), y_s.reshape(bs, 1, d), kvlat_p, krope_p, conv_state_p,
            kvlat_s.reshape(bs, 1, kvl), krope_s.reshape(bs, 1, rope), conv_state_s)
```

```python
import functools

import jax
import jax.numpy as jnp
from jax import lax
from jax.experimental import pallas as pl
from jax.experimental.pallas import tpu as pltpu

F32 = jnp.float32
BF16 = jnp.bfloat16
EPS = 1e-6
ROPE_THETA = 10000.0
LOG2E = 1.4426950408889634
LANE = 128
SUBLANE = 8
SLAB = 128
CONV_PAD = 32
VMEM_LIMIT_BYTES = 56 * 1024 * 1024


def _cparams(*sem):
    return pltpu.CompilerParams(dimension_semantics=sem, vmem_limit_bytes=VMEM_LIMIT_BYTES)


def _dot(a, b):
    return jnp.dot(a, b, preferred_element_type=F32)


def _dot_nt(a, b):
    return lax.dot_general(a, b, (((1,), (1,)), ((), ())), preferred_element_type=F32)


def _rms(x, g):
    return x * lax.rsqrt(jnp.mean(x * x, axis=-1, keepdims=True) + EPS) * g


def _silu(x):
    return x * jax.nn.sigmoid(x)


def _hilo(x):
    hi = x.astype(BF16)
    lo = (x - hi.astype(F32)).astype(BF16)
    return hi, lo


def _const_spec(shape):
    nd = len(shape)
    return pl.BlockSpec(shape, lambda *_: (0,) * nd, pipeline_mode=pl.Buffered(1))


def _ada_kernel(c_ref, w_ref, b_ref, o_ref):
    a = _silu(c_ref[...]).astype(BF16)
    o_ref[...] = _dot(a, w_ref[...].astype(BF16)) + b_ref[...]


def _ada(c_all, w_ada, b_ada):
    m, d = c_all.shape
    n = w_ada.shape[1]
    tn = n // 8
    return pl.pallas_call(
        _ada_kernel,
        grid=(n // tn,),
        in_specs=[pl.BlockSpec((m, d), lambda j: (0, 0)),
                  pl.BlockSpec((d, tn), lambda j: (0, j)),
                  pl.BlockSpec((1, tn), lambda j: (0, j))],
        out_specs=pl.BlockSpec((m, tn), lambda j: (0, j)),
        out_shape=jax.ShapeDtypeStruct((m, n), F32),
        compiler_params=_cparams("arbitrary"),
        name="ada",
    )(c_all, w_ada, b_ada.reshape(1, n))


def _rope_kernel(inv_ref, cq_ref, sq_ref, ck_ref, sk_ref, *, pos0):
    t = cq_ref.shape[0]
    pos = (lax.broadcasted_iota(jnp.int32, (t, LANE), 0) + pos0).astype(F32)
    aq = pos * inv_ref[0:1, :]
    ak = pos * inv_ref[1:2, :]
    cq_ref[...] = jnp.cos(aq)
    sq_ref[...] = jnp.sin(aq)
    ck_ref[...] = jnp.cos(ak)
    sk_ref[...] = jnp.sin(ak)


def _rope_tables(inv2, t, pos0):
    shp = jax.ShapeDtypeStruct((t, LANE), F32)
    return pl.pallas_call(
        functools.partial(_rope_kernel, pos0=pos0),
        out_shape=(shp, shp, shp, shp),
        name="rope_tables",
    )(inv2)


def _ffn_kernel(x_ref, sh_ref, sc_ref, gate_ref, g_ref, win_ref, wout_ref, o_ref, *, f, fc):
    x = x_ref[...]
    u = (_rms(x, g_ref[...]) * (1.0 + sc_ref[...]) + sh_ref[...]).astype(BF16)
    acc = None
    for c in range(f // fc):
        up = _dot(u, win_ref[:, c * fc:(c + 1) * fc])
        gt = _dot(u, win_ref[:, f + c * fc:f + (c + 1) * fc])
        a = (_silu(gt) * up).astype(BF16)
        part = _dot(a, wout_ref[c * fc:(c + 1) * fc, :])
        acc = part if acc is None else acc + part
    o_ref[...] = x + 0.5 * gate_ref[...] * acc


def _mod_spec(arr, tiles_per_seq):
    r, d = arr.shape[1:]
    return pl.BlockSpec((None, r, d), lambda i: (i // tiles_per_seq, 0, 0))


def _ffn(x, shift, scale, gate, g, w_in, w_out, tm, tiles_per_seq):
    m, d = x.shape
    f = w_out.shape[0]
    fc = f // 2
    return pl.pallas_call(
        functools.partial(_ffn_kernel, f=f, fc=fc),
        grid=(m // tm,),
        in_specs=[pl.BlockSpec((tm, d), lambda i: (i, 0)),
                  _mod_spec(shift, tiles_per_seq), _mod_spec(scale, tiles_per_seq),
                  _mod_spec(gate, tiles_per_seq),
                  _const_spec((1, d)), _const_spec(w_in.shape), _const_spec(w_out.shape)],
        out_specs=pl.BlockSpec((tm, d), lambda i: (i, 0)),
        out_shape=jax.ShapeDtypeStruct((m, d), F32),
        compiler_params=_cparams("arbitrary"),
        name="ffn",
    )(x, shift, scale, gate, g, w_in, w_out)


def _head_norm_store(x, gain, o_ref, n_heads, d_head):
    for h in range(n_heads):
        xh = x[:, h * SLAB:(h + 1) * SLAB]
        ss = jnp.sum(xh * xh, axis=-1, keepdims=True)
        rinv = lax.rsqrt(ss * (1.0 / d_head) + EPS)
        o_ref[:, h * SLAB:(h + 1) * SLAB] = (xh * rinv * gain).astype(o_ref.dtype)


def _q_heads(qn, wuq_ref, cq, sq, gain, o_ref, n_heads, d_head):
    hw = n_heads * SLAB
    group = 4
    for hc in range(n_heads // group):
        lo_, hi_ = hc * group * SLAB, (hc + 1) * group * SLAB
        a = _dot(qn, wuq_ref[:, lo_:hi_])
        r = _dot(qn, wuq_ref[:, hw + lo_:hw + hi_])
        for j in range(group):
            h = hc * group + j
            qh = a[:, j * SLAB:(j + 1) * SLAB] * cq + r[:, j * SLAB:(j + 1) * SLAB] * sq
            ss = jnp.sum(qh * qh, axis=-1, keepdims=True)
            rinv = lax.rsqrt(ss * (1.0 / d_head) + EPS)
            o_ref[:, h * SLAB:(h + 1) * SLAB] = (qh * rinv * gain).astype(o_ref.dtype)


def _latent_paths(u, wkv_ref, gkv_ref, ck, sk, kv_lora, rope):
    kvpe = _dot(u, wkv_ref[...])
    kv_lat = _rms(kvpe[:, :kv_lora], gkv_ref[...])
    kr = kvpe[:, kv_lora:kv_lora + LANE] * ck + kvpe[:, kv_lora + LANE:kv_lora + 2 * LANE] * sk
    return kv_lat, kr


def _key_ext(kv_lat_bf, kr, wuk_ref):
    kr_hi, kr_lo = _hilo(kr)
    x = jnp.concatenate([kv_lat_bf, kr_hi, kr_lo], axis=-1)
    return _dot(x, wuk_ref[...])


def _causal_dwconv(zb_ref, wdw_ref, bdw_ref, y_ref, tm, conv_w, conv_ch, cw=256):
    off = CONV_PAD - (conv_w - 1)
    nblk = (off + conv_w - 1) // SUBLANE + 1
    assert off > 0 and SUBLANE * (nblk - 1) + 1 - off >= conv_w
    rowi = lax.broadcasted_iota(jnp.int32, (SUBLANE, cw), 0)

    for cc in range(conv_ch // cw):
        cs = slice(cc * cw, (cc + 1) * cw)
        wrow = [wdw_ref[k:k + 1, cs] for k in range(conv_w)]
        bias = bdw_ref[:, cs]

        def phase_sum(s, blocks, first):
            acc = None
            for j in range(nblk):
                k = SUBLANE * j + s - off
                if 0 <= k < conv_w:
                    term = blocks[j - first] * wrow[k]
                    acc = term if acc is None else acc + term
            return acc

        def rolled(s, blocks):
            return pltpu.roll(phase_sum(s, blocks, 0), SUBLANE - s, 0)

        def load_blocks(first_row):
            return [zb_ref[pl.ds(first_row + SUBLANE * j, SUBLANE), cs] for j in range(nblk - 1)]

        carry0 = tuple(rolled(s, load_blocks(0)) for s in range(1, SUBLANE))

        def body(r, carry):
            base = r * SUBLANE
            blk = load_blocks(base + SUBLANE)
            y = phase_sum(0, blk, 1) + bias
            new = tuple(rolled(s, blk) for s in range(1, SUBLANE))
            for s in range(1, SUBLANE):
                y = y + jnp.where(rowi < SUBLANE - s, carry[s - 1], new[s - 1])
            y_ref[pl.ds(base, SUBLANE), cs] = y
            return new

        carry = carry0
        for r in range(tm // SUBLANE):
            carry = body(r, carry)


def _mixp_kernel(h_ref, sh_ref, sc_ref, gmix_ref, wq_ref, gql_ref, wuq_ref, gqh_ref,
                 wkv_ref, gkv_ref, wuk_ref, gkh_ref, wuvt_ref, wglu_ref,
                 cq_ref, sq_ref, ck_ref, sk_ref, wdw_ref, bdw_ref, gcn_ref, bcn_ref,
                 kvlat_ref, krope_ref, qx_ref, kx_ref, vt_ref, ycv_ref, cst_ref,
                 zb_ref, y_ref, *, n_heads, d_head, kv_lora, rope, conv_w, conv_ch):
    t = pl.program_id(1)
    tm = h_ref.shape[0]

    @pl.when((pl.program_id(0) == 0) & (t == 0))
    def _():
        zb_ref[0:CONV_PAD, :] = jnp.zeros((CONV_PAD, conv_ch), F32)

    u = (_rms(h_ref[...], gmix_ref[...]) * (1.0 + sc_ref[...]) + sh_ref[...]).astype(BF16)

    glu = _dot(u, wglu_ref[...])
    z = glu[:, :conv_ch] * jax.nn.sigmoid(glu[:, conv_ch:])
    zb_ref[0:CONV_PAD, :] = jnp.where(t == 0, 0.0, zb_ref[0:CONV_PAD, :])
    zb_ref[CONV_PAD:CONV_PAD + tm, :] = z
    _causal_dwconv(zb_ref, wdw_ref, bdw_ref, y_ref, tm, conv_w, conv_ch)
    tail = zb_ref[tm:tm + CONV_PAD, :]
    zb_ref[0:CONV_PAD, :] = tail
    cst_ref[...] = tail

    qn = _rms(_dot(u, wq_ref[...]), gql_ref[...]).astype(BF16)
    _q_heads(qn, wuq_ref, cq_ref[...], sq_ref[...], gqh_ref[...], qx_ref, n_heads, d_head)

    kv_lat, kr = _latent_paths(u, wkv_ref, gkv_ref, ck_ref[...], sk_ref[...], kv_lora, rope)
    kvlat_ref[...] = kv_lat
    krope_ref[...] = kr[:, :rope]
    kv_bf = kv_lat.astype(BF16)
    _head_norm_store(_key_ext(kv_bf, kr, wuk_ref), gkh_ref[...], kx_ref, n_heads, d_head)
    vt_ref[...] = _dot_nt(wuvt_ref[...], kv_bf).astype(BF16)

    y = y_ref[...]
    mu = jnp.mean(y, axis=-1, keepdims=True)
    yc = y - mu
    var = jnp.mean(yc * yc, axis=-1, keepdims=True)
    yn = yc * lax.rsqrt(var + EPS) * gcn_ref[...] + bcn_ref[...]
    ycv_ref[...] = _silu(yn).astype(BF16)


def _mixer_prompt(h, shift, scale, wts, tabs, dims, tm):
    b, t, d = h.shape
    nh, dh, kvl, rope, cw_, cch = dims
    nt = t // tm
    hw = nh * SLAB
    kern = functools.partial(_mixp_kernel, n_heads=nh, d_head=dh, kv_lora=kvl, rope=rope,
                             conv_w=cw_, conv_ch=cch)
    tok = lambda n: pl.BlockSpec((None, tm, n), lambda i, j: (i, j, 0))
    mod = pl.BlockSpec((None, 1, d), lambda i, j: (i, 0, 0))
    tab = pl.BlockSpec((tm, LANE), lambda i, j: (j, 0))
    wnames = ("g_mix", "wq", "g_q_lat", "wuq", "gqh_exp2", "wkv", "g_kv_lat", "wuk", "gkh", "wuvt", "wglu")
    cnames = ("wdw", "b_dw", "g_cn", "b_cn")
    vdim = wts["wuvt"].shape[0]
    in_specs = ([tok(d), mod, mod] + [_const_spec(wts[n].shape) for n in wnames] + [tab] * 4
                + [_const_spec(wts[n].shape) for n in cnames])
    out_shape = (jax.ShapeDtypeStruct((b, t, kvl), F32), jax.ShapeDtypeStruct((b, t, rope), F32),
                 jax.ShapeDtypeStruct((b, t, hw), BF16), jax.ShapeDtypeStruct((b, t, hw), BF16),
                 jax.ShapeDtypeStruct((b, vdim, t), BF16), jax.ShapeDtypeStruct((b, t, cch), BF16),
                 jax.ShapeDtypeStruct((b, CONV_PAD, cch), F32))
    out_specs = (tok(kvl), tok(rope), tok(hw), tok(hw),
                 pl.BlockSpec((None, vdim, tm), lambda i, j: (i, 0, j)), tok(cch),
                 pl.BlockSpec((None, CONV_PAD, cch), lambda i, j: (i, 0, 0)))
    return pl.pallas_call(
        kern, grid=(b, nt), in_specs=in_specs, out_specs=out_specs, out_shape=out_shape,
        scratch_shapes=[pltpu.VMEM((tm + CONV_PAD, cch), F32), pltpu.VMEM((tm, cch), F32)],
        compiler_params=_cparams("arbitrary", "arbitrary"),
        name="mixer_prompt",
    )(h, shift, scale, *[wts[n] for n in wnames], *tabs, *[wts[n] for n in cnames])


def _attn_kernel(q_ref, k_ref, vt_ref, o_ref, *, tq, d_v):
    t = q_ref.shape[0]
    kv_i = lax.broadcasted_iota(jnp.int32, (tq, tq), 0)
    q_i = lax.broadcasted_iota(jnp.int32, (tq, tq), 1)
    causal = kv_i <= q_i
    ones = jnp.ones((2 * SUBLANE, t), BF16)
    for qi in range(t // tq):
        rows = slice(qi * tq, (qi + 1) * tq)
        n_kv = (qi + 1) * tq
        vt = jnp.concatenate([vt_ref[:, 0:n_kv], ones[:, 0:n_kv]], axis=0)
        halves = []
        for hh in range(2):
            lanes = slice(hh * SLAB, (hh + 1) * SLAB)
            q = q_ref[rows, lanes]
            sd = jnp.where(causal, _dot_nt(k_ref[rows, lanes], q), -jnp.inf)
            m = jnp.max(sd, axis=0, keepdims=True)
            if qi > 0:
                so = _dot_nt(k_ref[0:qi * tq, lanes], q)
                m = jnp.maximum(m, jnp.max(so, axis=0, keepdims=True))
                p = jnp.concatenate([jnp.exp2(so - m).astype(BF16), jnp.exp2(sd - m).astype(BF16)], axis=0)
            else:
                p = jnp.exp2(sd - m).astype(BF16)
            ot = _dot(vt, p)
            halves.append(ot[hh * d_v:(hh + 1) * d_v, :] / ot[2 * d_v:2 * d_v + 1, :])
        o_ref[rows, :] = jnp.concatenate(halves, axis=0).T.astype(o_ref.dtype)


def _attn_prompt(qx, kx, vt, n_heads, d_v, tq):
    b, t, _ = qx.shape
    pairs = n_heads // 2
    return pl.pallas_call(
        functools.partial(_attn_kernel, tq=tq, d_v=d_v),
        grid=(b, pairs),
        in_specs=[pl.BlockSpec((None, t, 2 * SLAB), lambda i, p: (i, 0, p)),
                  pl.BlockSpec((None, t, 2 * SLAB), lambda i, p: (i, 0, p)),
                  pl.BlockSpec((None, 2 * d_v, t), lambda i, p: (i, p, 0))],
        out_specs=pl.BlockSpec((None, t, 2 * d_v), lambda i, p: (i, 0, p)),
        out_shape=jax.ShapeDtypeStruct((b, t, n_heads * d_v), BF16),
        compiler_params=_cparams("arbitrary", "arbitrary"),
        name="attn_prompt",
    )(qx, kx, vt)


def _merge_kernel(h_ref, sh_ref, sc_ref, gate_ref, g_ref, o_ref, y_ref,
                  wg_ref, wao_ref, wco_ref, wout_ref, out_ref):
    h = h_ref[...]
    u = (_rms(h, g_ref[...]) * (1.0 + sc_ref[...]) + sh_ref[...]).astype(BF16)
    gl = _dot(u, wg_ref[...])
    d = wao_ref.shape[1]
    a = _dot(o_ref[...], wao_ref[...])
    b = _dot(y_ref[...], wco_ref[...])
    merged = jax.nn.sigmoid(gl[:, :d]) * a + jax.nn.sigmoid(gl[:, d:]) * b
    out_ref[...] = h + gate_ref[...] * _dot(merged.astype(BF16), wout_ref[...])


def _merge(h, shift, scale, gate, g, o, ycv, wg, wao, wco, wout, tm, tiles_per_seq):
    m, d = h.shape
    row = lambda n: pl.BlockSpec((tm, n), lambda i: (i, 0))
    return pl.pallas_call(
        _merge_kernel,
        grid=(m // tm,),
        in_specs=[row(d), _mod_spec(shift, tiles_per_seq), _mod_spec(scale, tiles_per_seq),
                  _mod_spec(gate, tiles_per_seq), _const_spec((1, d)), row(o.shape[1]), row(ycv.shape[1]),
                  _const_spec(wg.shape), _const_spec(wao.shape), _const_spec(wco.shape),
                  _const_spec(wout.shape)],
        out_specs=row(d),
        out_shape=jax.ShapeDtypeStruct((m, d), F32),
        compiler_params=_cparams("arbitrary"),
        name="merge",
    )(h, shift, scale, gate, g, o, ycv, wg, wao, wco, wout)


def _mixs_kernel(h_ref, sh_ref, sc_ref, gmix_ref, wq_ref, gql_ref, wuq_ref, gqh_ref,
                 wkv_ref, gkv_ref, wuk_ref, gkh_ref, wukt_ref, wglu_ref,
                 cq_ref, sq_ref, ck_ref, sk_ref, wdw_ref, bdw_ref, gcn_ref, bcn_ref, st_ref,
                 kvlat_ref, krope_ref, qx_ref, qg_ref, qt_ref, knew_ref, z_ref, ycv_ref,
                 *, n_heads, d_head, kv_lora, rope, conv_w, conv_ch):
    u = (_rms(h_ref[...], gmix_ref[...]) * (1.0 + sc_ref[...]) + sh_ref[...]).astype(BF16)

    qn = _rms(_dot(u, wq_ref[...]), gql_ref[...]).astype(BF16)
    _q_heads(qn, wuq_ref, cq_ref[0:1, :], sq_ref[0:1, :], gqh_ref[...], qx_ref, n_heads, d_head)
    for h in range(n_heads):
        qg = qx_ref[:, h * SLAB:(h + 1) * SLAB] * gkh_ref[...]
        qg_ref[:, h * SLAB:(h + 1) * SLAB] = qg
        qt_ref[:, h * kv_lora:(h + 1) * kv_lora] = _dot(qg.astype(BF16), wukt_ref[h])

    kv_lat, kr = _latent_paths(u, wkv_ref, gkv_ref, ck_ref[0:1, :], sk_ref[0:1, :], kv_lora, rope)
    kvlat_ref[...] = kv_lat
    krope_ref[...] = kr[:, :rope]
    _head_norm_store(_key_ext(kv_lat.astype(BF16), kr, wuk_ref), gkh_ref[...], knew_ref, n_heads, d_head)

    glu = _dot(u, wglu_ref[...])
    z = glu[:, :conv_ch] * jax.nn.sigmoid(glu[:, conv_ch:])
    z_ref[...] = z
    y = z * wdw_ref[conv_w - 1:conv_w, :] + bdw_ref[...]
    for k in range(conv_w - 1):
        y = y + st_ref[k] * wdw_ref[k:k + 1, :]
    mu = jnp.mean(y, axis=-1, keepdims=True)
    yc = y - mu
    var = jnp.mean(yc * yc, axis=-1, keepdims=True)
    yn = yc * lax.rsqrt(var + EPS) * gcn_ref[...] + bcn_ref[...]
    ycv_ref[...] = _silu(yn).astype(BF16)


def _mixer_sample(h, shift, scale, wts, tabs, state, dims, tm):
    m, d = h.shape
    nh, dh, kvl, rope, cw_, cch = dims
    hw = nh * SLAB
    kern = functools.partial(_mixs_kernel, n_heads=nh, d_head=dh, kv_lora=kvl, rope=rope,
                             conv_w=cw_, conv_ch=cch)
    row = lambda n: pl.BlockSpec((tm, n), lambda i: (i, 0))
    tab = pl.BlockSpec((8, LANE), lambda i: (0, 0))
    wnames = ("g_mix", "wq", "g_q_lat", "wuq", "gqh", "wkv", "g_kv_lat", "wuk", "gkh", "wukt", "wglu")
    cnames = ("wdw", "b_dw", "g_cn", "b_cn")
    in_specs = ([row(d), row(d), row(d)] + [_const_spec(wts[n].shape) for n in wnames] + [tab] * 4
                + [_const_spec(wts[n].shape) for n in cnames]
                + [pl.BlockSpec((cw_ - 1, tm, cch), lambda i: (0, i, 0))])
    out_shape = (jax.ShapeDtypeStruct((m, kvl), F32), jax.ShapeDtypeStruct((m, rope), F32),
                 jax.ShapeDtypeStruct((m, hw), F32), jax.ShapeDtypeStruct((m, hw), F32),
                 jax.ShapeDtypeStruct((m, nh * kvl), F32), jax.ShapeDtypeStruct((m, hw), F32),
                 jax.ShapeDtypeStruct((m, cch), F32), jax.ShapeDtypeStruct((m, cch), BF16))
    out_specs = (row(kvl), row(rope), row(hw), row(hw), row(nh * kvl), row(hw), row(cch), row(cch))
    return pl.pallas_call(
        kern, grid=(m // tm,), in_specs=in_specs, out_specs=out_specs, out_shape=out_shape,
        compiler_params=_cparams("arbitrary"),
        name="mixer_sample",
    )(h, shift, scale, *[wts[n] for n in wnames], *tabs, *[wts[n] for n in cnames], state)


def _paged_kernel(pt_ref, qg_ref, qt_ref, qx_ref, knew_ref, cnew_ref, wukt_ref, kv_hbm, kr_hbm,
                  o_ref, kvbuf, krbuf, sem, cbf, s_scr, m_scr, l_scr, acc_scr,
                  *, pages, page, tile, n_heads, d_head, d_nope, rope):
    b = pl.program_id(0)
    c2 = pl.program_id(1)
    nc = 2 * pl.num_programs(1)
    nsteps = pl.num_programs(0) * nc
    npos = pages * page

    def copies(step_i, slot_i):
        out = []
        for p in range(pages):
            pg = pt_ref[step_i * pages + p]
            out.append(pltpu.make_async_copy(
                kv_hbm.at[pg], kvbuf.at[slot_i, pl.ds(p * page, page), :], sem.at[0, slot_i]))
            out.append(pltpu.make_async_copy(
                kr_hbm.at[pg], krbuf.at[slot_i, :, pl.ds(p * page, page)], sem.at[1, slot_i]))
        return out

    def softmax_update(m_prev, l_prev, acc_prev, s, cb):
        m_new = jnp.maximum(m_prev, jnp.max(s, axis=-1, keepdims=True))
        corr = jnp.exp(m_prev - m_new)
        p = jnp.exp(s - m_new)
        l_new = l_prev * corr + jnp.sum(p, axis=-1, keepdims=True)
        acc_new = acc_prev * corr + _dot(p.astype(BF16), cb)
        return m_new, l_new, acc_new

    @pl.when((b == 0) & (c2 == 0))
    def _():
        for cp in copies(0, 0):
            cp.start()
        s_scr[...] = jnp.zeros(s_scr.shape, F32)
        cbf[...] = jnp.zeros(cbf.shape, BF16)
        m_scr[...] = jnp.zeros(m_scr.shape, F32)
        l_scr[...] = jnp.zeros(l_scr.shape, F32)
        acc_scr[...] = jnp.zeros(acc_scr.shape, F32)

    qt_hi, qt_lo = _hilo(qt_ref[...])
    qtl = jnp.concatenate([qt_hi, qt_lo], axis=0)
    qg_hi, qg_lo = _hilo(qg_ref[:, d_nope:d_head])
    qgl = jnp.concatenate([qg_hi, qg_lo], axis=0)

    def chunk(slot):
        c = 2 * c2 + slot
        step = b * nc + c
        for cp in copies(step, slot):
            cp.wait()
        nxt = jnp.minimum(step + 1, nsteps - 1)
        for cp in copies(nxt, 1 - slot):
            cp.start()

        m_u, l_u, acc_u = softmax_update(m_scr[...], l_scr[...], acc_scr[...],
                                         s_scr[1 - slot], cbf[1 - slot])
        if slot == 0:
            fresh = c2 == 0
            m_u = jnp.where(fresh, -jnp.inf, m_u)
            l_u = jnp.where(fresh, 0.0, l_u)
            acc_u = jnp.where(fresh, 0.0, acc_u)
        m_scr[...] = m_u
        l_scr[...] = l_u
        acc_scr[...] = acc_u

        krt = krbuf[slot]
        krss = jnp.sum(krt * krt, axis=0, keepdims=True)
        sr = _dot(qgl, krt.astype(BF16))
        s_rope = sr[:n_heads] + sr[n_heads:]
        for j in range(npos // tile):
            cols = slice(j * tile, (j + 1) * tile)
            cb = kvbuf[slot, cols, :].astype(BF16)
            cbf[slot, cols, :] = cb
            kn = _dot_nt(wukt_ref[...], cb)
            ssn = jnp.sum((kn * kn).reshape(n_heads, d_nope, tile), axis=1)
            sn = _dot_nt(qtl, cb)
            rinv = lax.rsqrt((ssn + krss[:, cols]) * (1.0 / d_head) + EPS)
            s_scr[slot, :, cols] = (sn[:n_heads] + sn[n_heads:] + s_rope[:, cols]) * rinv
        return nxt

    chunk(0)
    nxt_last = chunk(1)

    @pl.when((b == pl.num_programs(0) - 1) & (c2 == pl.num_programs(1) - 1))
    def _():
        for cp in copies(nxt_last, 0):
            cp.wait()

    @pl.when(c2 == pl.num_programs(1) - 1)
    def _():
        m_new, l_new, acc_new = softmax_update(m_scr[...], l_scr[...], acc_scr[...], s_scr[1], cbf[1])
        s_self = jnp.sum(qx_ref[...] * knew_ref[...], axis=-1, keepdims=True)
        m2 = jnp.maximum(m_new, s_self)
        corr2 = jnp.exp(m_new - m2)
        p_self = jnp.exp(s_self - m2)
        l2 = l_new * corr2 + p_self
        o_ref[...] = (acc_new * corr2 + p_self * cnew_ref[...]) / l2


def _paged_attn(page_table, qg, qt, qx, knew, cnew, wukt, cache_kv, cache_kr, dims, pages, tile):
    nb, n_pages = page_table.shape
    nh, dh, kvl, rope = dims
    d_nope = dh - rope
    page = cache_kv.shape[1]
    assert n_pages % (2 * pages) == 0
    nc2 = n_pages // (2 * pages)
    npos = pages * page
    kern = functools.partial(_paged_kernel, pages=pages, page=page, tile=tile, n_heads=nh,
                             d_head=dh, d_nope=d_nope, rope=rope)
    per_b = lambda r, n: pl.BlockSpec((None, r, n), lambda i, j, pt: (i, 0, 0))
    grid_spec = pltpu.PrefetchScalarGridSpec(
        num_scalar_prefetch=1,
        grid=(nb, nc2),
        in_specs=[per_b(nh, SLAB), per_b(nh, kvl), per_b(nh, SLAB), per_b(nh, SLAB), per_b(1, kvl),
                  pl.BlockSpec(wukt.shape, lambda i, j, pt: (0, 0)),
                  pl.BlockSpec(memory_space=pl.ANY), pl.BlockSpec(memory_space=pl.ANY)],
        out_specs=per_b(nh, kvl),
        scratch_shapes=[pltpu.VMEM((2, npos, kvl), F32), pltpu.VMEM((2, rope, npos), F32),
                        pltpu.SemaphoreType.DMA((2, 2)),
                        pltpu.VMEM((2, npos, kvl), BF16), pltpu.VMEM((2, nh, npos), F32),
                        pltpu.VMEM((nh, 1), F32), pltpu.VMEM((nh, 1), F32), pltpu.VMEM((nh, kvl), F32)])
    return pl.pallas_call(
        kern, grid_spec=grid_spec,
        out_shape=jax.ShapeDtypeStruct((nb, nh, kvl), F32),
        compiler_params=_cparams("arbitrary", "arbitrary"),
        name="paged_attn",
    )(page_table.reshape(-1), qg, qt, qx, knew, cnew, wukt, cache_kv, cache_kr)


def _uvup_kernel(ol_ref, wa_ref, wb_ref, o_ref):
    for p in range(wa_ref.shape[0]):
        a = _dot(ol_ref[2 * p].astype(BF16), wa_ref[p])
        b = _dot(ol_ref[2 * p + 1].astype(BF16), wb_ref[p])
        o_ref[:, p * LANE:(p + 1) * LANE] = (a + b).astype(o_ref.dtype)


def _uv_up(olat_t, wa, wb):
    nh, m, _ = olat_t.shape
    return pl.pallas_call(
        _uvup_kernel,
        out_shape=jax.ShapeDtypeStruct((m, wa.shape[0] * LANE), BF16),
        compiler_params=pltpu.CompilerParams(vmem_limit_bytes=VMEM_LIMIT_BYTES),
        name="uv_up",
    )(olat_t, wa, wb)


def _prep_weights(w_in, g_norm_mix, g_q_lat, w_uq, g_q_head, g_kv_lat, w_uk, w_uv, g_k_head,
                  w_dw, b_dw, g_conv_norm, b_conv_norm):
    d = w_in.shape[0]
    q_lora, nh, dh = w_uq.shape
    kvl, _, d_nope = w_uk.shape
    rope = dh - d_nope
    half = rope // 2
    conv_w, cch = w_dw.shape
    pad = SLAB - dh
    c0, c1, c2, c3 = q_lora, q_lora + kvl, q_lora + kvl + rope, q_lora + kvl + rope + 2 * cch

    def rot_half(x):
        return jnp.concatenate([-x[..., half:], x[..., :half]], axis=-1)

    pe = w_in[:, c1:c2]
    zpad = jnp.zeros((d, LANE - rope), F32)
    wkv = jnp.concatenate([w_in[:, c0:c1], pe, zpad, rot_half(pe), zpad], axis=1)

    wuq_a = jnp.pad(w_uq, ((0, 0), (0, 0), (0, pad)))
    wuq_r = jnp.pad(rot_half(w_uq[..., d_nope:]), ((0, 0), (0, 0), (d_nope, pad)))
    wuq = jnp.concatenate([wuq_a.reshape(q_lora, nh * SLAB), wuq_r.reshape(q_lora, nh * SLAB)], axis=1)

    sel = jnp.zeros((rope, nh, SLAB), F32).at[:, :, d_nope:dh].set(
        jnp.broadcast_to(jnp.eye(rope, dtype=F32)[:, None, :], (rope, nh, rope)))
    sel = jnp.pad(sel.reshape(rope, nh * SLAB), ((0, LANE - rope), (0, 0)))
    wuk = jnp.concatenate([jnp.pad(w_uk, ((0, 0), (0, 0), (0, SLAB - d_nope))).reshape(kvl, nh * SLAB),
                           sel, sel], axis=0)
    wukt = jnp.pad(jnp.transpose(w_uk, (1, 2, 0)), ((0, 0), (0, SLAB - d_nope), (0, 0)))

    scale = dh ** -0.5
    return dict(
        g_mix=g_norm_mix.reshape(1, d), wq=w_in[:, :c0].astype(BF16), g_q_lat=g_q_lat.reshape(1, q_lora),
        wuq=wuq.astype(BF16), gqh=(jnp.pad(g_q_head, (0, pad)) * scale).reshape(1, SLAB),
        wkv=wkv.astype(BF16), g_kv_lat=g_kv_lat.reshape(1, kvl), wuk=wuk.astype(BF16),
        gqh_exp2=(jnp.pad(g_q_head, (0, pad)) * (scale * LOG2E)).reshape(1, SLAB),
        gkh=jnp.pad(g_k_head, (0, pad)).reshape(1, SLAB),
        wuvt=jnp.transpose(w_uv.reshape(kvl, -1)).astype(BF16),
        wukt=wukt.astype(BF16),
        wukt_flat=jnp.transpose(w_uk, (1, 2, 0)).reshape(nh * d_nope, kvl).astype(BF16),
        wglu=w_in[:, c2:c3].astype(BF16), wgate=w_in[:, c3:].astype(BF16),
        wdw=jnp.pad(w_dw, ((0, CONV_PAD - conv_w), (0, 0))), b_dw=b_dw.reshape(1, cch),
        g_cn=g_conv_norm.reshape(1, cch), b_cn=b_conv_norm.reshape(1, cch))


def kernel(x_prompt, x_sample, c_prompt, c_sample, cache_kv_latent, cache_k_rope, state_conv, page_table, w_ada, b_ada, g_norm_ffn1, w_ffn1_in, w_ffn1_out, g_norm_mix, w_in, g_q_lat, w_uq, g_q_head, g_kv_lat, w_uk, w_uv, g_k_head, w_attn_out, w_dw, b_dw, g_conv_norm, b_conv_norm, w_conv_out, w_out, g_norm_ffn2, w_ffn2_in, w_ffn2_out):
    bp, t, d = x_prompt.shape
    bs, ts, _ = x_sample.shape
    assert ts == 1, "sample group handles one new token per sequence"
    q_lora, nh, dh = w_uq.shape
    kvl, _, d_nope = w_uk.shape
    d_v = w_uv.shape[2]
    rope = dh - d_nope
    conv_w, cch = w_dw.shape
    n_pages = page_table.shape[1]
    page = cache_kv_latent.shape[1]
    past_len = n_pages * page
    dims = (nh, dh, kvl, rope, conv_w, cch)

    tm_ffn = min(512, t)
    tm_mix = min(256, t)
    tm_merge = min(512, t)
    tq = min(512, t)
    tm_s = 32
    pages_per_step = min(16, n_pages)
    pos_tile = 256

    wts = _prep_weights(w_in, g_norm_mix, g_q_lat, w_uq, g_q_head, g_kv_lat, w_uk, w_uv, g_k_head,
                        w_dw, b_dw, g_conv_norm, b_conv_norm)
    w1i, w1o = w_ffn1_in.astype(BF16), w_ffn1_out.astype(BF16)
    w2i, w2o = w_ffn2_in.astype(BF16), w_ffn2_out.astype(BF16)
    wao = w_attn_out.reshape(nh * d_v, d).astype(BF16)
    wco = w_conv_out.astype(BF16)
    wo = w_out.astype(BF16)
    g1, g2 = g_norm_ffn1.reshape(1, d), g_norm_ffn2.reshape(1, d)

    mod = _ada(jnp.concatenate([c_prompt, c_sample], axis=0), w_ada, b_ada)
    n_mod = mod.shape[1] // d
    mod_p = [mod[:bp, i * d:(i + 1) * d].reshape(bp, 1, d) for i in range(n_mod)]
    mod_s = [mod[bp:, i * d:(i + 1) * d] for i in range(n_mod)]
    mod_s3 = [m_.reshape(1, bs, d) for m_ in mod_s]

    half = rope // 2
    inv = ROPE_THETA ** (-jnp.arange(half, dtype=F32) / half)
    inv_q = jnp.zeros((LANE,), F32).at[d_nope:dh].set(jnp.tile(inv, 2))
    inv_k = jnp.zeros((LANE,), F32).at[:rope].set(jnp.tile(inv, 2))
    inv2 = jnp.stack([inv_q, inv_k])
    tabs_p = _rope_tables(inv2, t, 0)
    tabs_s = _rope_tables(inv2, 8, past_len)

    xp = x_prompt.reshape(bp * t, d)
    hp = _ffn(xp, mod_p[0], mod_p[1], mod_p[2], g1, w1i, w1o, tm_ffn, t // tm_ffn)
    kvlat_p, krope_p, qx, kx, v, ycv_p, cst = _mixer_prompt(
        hp.reshape(bp, t, d), mod_p[3], mod_p[4], wts, tabs_p, dims, tm_mix)
    o_p = _attn_prompt(qx, kx, v, nh, d_v, tq)
    h2p = _merge(hp, mod_p[3], mod_p[4], mod_p[5], wts["g_mix"], o_p.reshape(bp * t, nh * d_v),
                 ycv_p.reshape(bp * t, cch), wts["wgate"], wao, wco, wo, tm_merge, t // tm_merge)
    y_p = _ffn(h2p, mod_p[6], mod_p[7], mod_p[8], g2, w2i, w2o, tm_ffn, t // tm_ffn)
    conv_state_p = cst[:, CONV_PAD - (conv_w - 1):, :]

    xs = x_sample.reshape(bs, d)
    hs = _ffn(xs, mod_s3[0], mod_s3[1], mod_s3[2], g1, w1i, w1o, bs, 1)
    kvlat_s, krope_s, qx_s, qg_s, qt_s, knew_s, z_s, ycv_s = _mixer_sample(
        hs, mod_s[3], mod_s[4], wts, tabs_s, jnp.transpose(state_conv, (1, 0, 2)), dims, tm_s)
    olat = _paged_attn(page_table, qg_s.reshape(bs, nh, SLAB), qt_s.reshape(bs, nh, kvl),
                       qx_s.reshape(bs, nh, SLAB), knew_s.reshape(bs, nh, SLAB),
                       kvlat_s.reshape(bs, 1, kvl), wts["wukt_flat"], cache_kv_latent,
                       jnp.swapaxes(cache_k_rope, 1, 2),
                       (nh, dh, kvl, rope), pages_per_step, pos_tile)
    wuv_pair = w_uv.reshape(kvl, nh // 2, 2, d_v)
    zero = jnp.zeros((kvl, nh // 2, d_v), F32)
    wuv_a = jnp.transpose(jnp.concatenate([wuv_pair[:, :, 0], zero], axis=-1), (1, 0, 2)).astype(BF16)
    wuv_b = jnp.transpose(jnp.concatenate([zero, wuv_pair[:, :, 1]], axis=-1), (1, 0, 2)).astype(BF16)
    o_s = _uv_up(jnp.transpose(olat, (1, 0, 2)), wuv_a, wuv_b)
    h2s = _merge(hs, mod_s3[3], mod_s3[4], mod_s3[5], wts["g_mix"], o_s, ycv_s,
                 wts["wgate"], wao, wco, wo, bs, 1)
    y_s = _ffn(h2s, mod_s3[6], mod_s3[7], mod_s3[8], g2, w2i, w2o, bs, 1)
    conv_state_s = jnp.concatenate([state_conv[:, 1:, :], z_s[:, None, :]], axis=1)

    return (y_p.reshape(bp, t, d), y_s.reshape(bs, 1, d), kvlat_p, krope_p, conv_state_p,
            kvlat_s.reshape(bs, 1, kvl), krope_s.reshape(bs, 1, rope), conv_state_s)
```

```python
import functools

import jax
import jax.numpy as jnp
from jax import lax
from jax.experimental import pallas as pl
from jax.experimental.pallas import tpu as pltpu

F32 = jnp.float32
BF16 = jnp.bfloat16
EPS = 1e-6
ROPE_THETA = 10000.0
LOG2E = 1.4426950408889634
LANE = 128
SUBLANE = 8
SLAB = 128
CONV_PAD = 32
VMEM_LIMIT_BYTES = 56 * 1024 * 1024


def _cparams(*sem):
    return pltpu.CompilerParams(dimension_semantics=sem, vmem_limit_bytes=VMEM_LIMIT_BYTES)


def _dot(a, b):
    return jnp.dot(a, b, preferred_element_type=F32)


def _dot_nt(a, b):
    return lax.dot_general(a, b, (((1,), (1,)), ((), ())), preferred_element_type=F32)


def _rms(x, g):
    return x * lax.rsqrt(jnp.mean(x * x, axis=-1, keepdims=True) + EPS) * g


def _silu(x):
    return x * jax.nn.sigmoid(x)


def _hilo(x):
    hi = x.astype(BF16)
    lo = (x - hi.astype(F32)).astype(BF16)
    return hi, lo


def _const_spec(shape):
    nd = len(shape)
    return pl.BlockSpec(shape, lambda *_: (0,) * nd, pipeline_mode=pl.Buffered(1))


def _ada_kernel(c_ref, w_ref, b_ref, o_ref):
    a = _silu(c_ref[...]).astype(BF16)
    o_ref[...] = _dot(a, w_ref[...].astype(BF16)) + b_ref[...]


def _ada(c_all, w_ada, b_ada):
    m, d = c_all.shape
    n = w_ada.shape[1]
    tn = n // 8
    return pl.pallas_call(
        _ada_kernel,
        grid=(n // tn,),
        in_specs=[pl.BlockSpec((m, d), lambda j: (0, 0)),
                  pl.BlockSpec((d, tn), lambda j: (0, j)),
                  pl.BlockSpec((1, tn), lambda j: (0, j))],
        out_specs=pl.BlockSpec((m, tn), lambda j: (0, j)),
        out_shape=jax.ShapeDtypeStruct((m, n), F32),
        compiler_params=_cparams("arbitrary"),
        name="ada",
    )(c_all, w_ada, b_ada.reshape(1, n))


def _rope_kernel(inv_ref, cq_ref, sq_ref, ck_ref, sk_ref, *, pos0):
    t = cq_ref.shape[0]
    pos = (lax.broadcasted_iota(jnp.int32, (t, LANE), 0) + pos0).astype(F32)
    aq = pos * inv_ref[0:1, :]
    ak = pos * inv_ref[1:2, :]
    cq_ref[...] = jnp.cos(aq)
    sq_ref[...] = jnp.sin(aq)
    ck_ref[...] = jnp.cos(ak)
    sk_ref[...] = jnp.sin(ak)


def _rope_tables(inv2, t, pos0):
    shp = jax.ShapeDtypeStruct((t, LANE), F32)
    return pl.pallas_call(
        functools.partial(_rope_kernel, pos0=pos0),
        out_shape=(shp, shp, shp, shp),
        name="rope_tables",
    )(inv2)


def _ffn_kernel(x_ref, sh_ref, sc_ref, gate_ref, g_ref, win_ref, wout_ref, o_ref, *, f, fc):
    x = x_ref[...]
    u = (_rms(x, g_ref[...]) * (1.0 + sc_ref[...]) + sh_ref[...]).astype(BF16)
    acc = None
    for c in range(f // fc):
        up = _dot(u, win_ref[:, c * fc:(c + 1) * fc])
        gt = _dot(u, win_ref[:, f + c * fc:f + (c + 1) * fc])
        a = (_silu(gt) * up).astype(BF16)
        part = _dot(a, wout_ref[c * fc:(c + 1) * fc, :])
        acc = part if acc is None else acc + part
    o_ref[...] = x + 0.5 * gate_ref[...] * acc


def _mod_spec(arr, tiles_per_seq):
    r, d = arr.shape[1:]
    return pl.BlockSpec((None, r, d), lambda i: (i // tiles_per_seq, 0, 0))


def _ffn(x, shift, scale, gate, g, w_in, w_out, tm, tiles_per_seq):
    m, d = x.shape
    f = w_out.shape[0]
    fc = f // 2
    return pl.pallas_call(
        functools.partial(_ffn_kernel, f=f, fc=fc),
        grid=(m // tm,),
        in_specs=[pl.BlockSpec((tm, d), lambda i: (i, 0)),
                  _mod_spec(shift, tiles_per_seq), _mod_spec(scale, tiles_per_seq),
                  _mod_spec(gate, tiles_per_seq),
                  _const_spec((1, d)), _const_spec(w_in.shape), _const_spec(w_out.shape)],
        out_specs=pl.BlockSpec((tm, d), lambda i: (i, 0)),
        out_shape=jax.ShapeDtypeStruct((m, d), F32),
        compiler_params=_cparams("arbitrary"),
        name="ffn",
    )(x, shift, scale, gate, g, w_in, w_out)


def _head_norm_store(x, gain, o_ref, n_heads, d_head):
    for h in range(n_heads):
        xh = x[:, h * SLAB:(h + 1) * SLAB]
        ss = jnp.sum(xh * xh, axis=-1, keepdims=True)
        rinv = lax.rsqrt(ss * (1.0 / d_head) + EPS)
        o_ref[:, h * SLAB:(h + 1) * SLAB] = (xh * rinv * gain).astype(o_ref.dtype)


def _q_heads(qn, wuq_ref, cq, sq, gain, o_ref, n_heads, d_head):
    hw = n_heads * SLAB
    group = 4
    for hc in range(n_heads // group):
        lo_, hi_ = hc * group * SLAB, (hc + 1) * group * SLAB
        a = _dot(qn, wuq_ref[:, lo_:hi_])
        r = _dot(qn, wuq_ref[:, hw + lo_:hw + hi_])
        for j in range(group):
            h = hc * group + j
            qh = a[:, j * SLAB:(j + 1) * SLAB] * cq + r[:, j * SLAB:(j + 1) * SLAB] * sq
            ss = jnp.sum(qh * qh, axis=-1, keepdims=True)
            rinv = lax.rsqrt(ss * (1.0 / d_head) + EPS)
            o_ref[:, h * SLAB:(h + 1) * SLAB] = (qh * rinv * gain).astype(o_ref.dtype)


def _latent_paths(u, wkv_ref, gkv_ref, ck, sk, kv_lora, rope):
    kvpe = _dot(u, wkv_ref[...])
    kv_lat = _rms(kvpe[:, :kv_lora], gkv_ref[...])
    kr = kvpe[:, kv_lora:kv_lora + LANE] * ck + kvpe[:, kv_lora + LANE:kv_lora + 2 * LANE] * sk
    return kv_lat, kr


def _key_ext(kv_lat_bf, kr, wuk_ref):
    kr_hi, kr_lo = _hilo(kr)
    x = jnp.concatenate([kv_lat_bf, kr_hi, kr_lo], axis=-1)
    return _dot(x, wuk_ref[...])


def _causal_dwconv(zb_ref, wdw_ref, bdw_ref, y_ref, tm, conv_w, conv_ch, cw=256):
    off = CONV_PAD - (conv_w - 1)
    nblk = (off + conv_w - 1) // SUBLANE + 1
    assert off > 0 and SUBLANE * (nblk - 1) + 1 - off >= conv_w
    rowi = lax.broadcasted_iota(jnp.int32, (SUBLANE, cw), 0)

    for cc in range(conv_ch // cw):
        cs = slice(cc * cw, (cc + 1) * cw)
        wrow = [wdw_ref[k:k + 1, cs] for k in range(conv_w)]
        bias = bdw_ref[:, cs]

        def phase_sum(s, blocks, first):
            acc = None
            for j in range(nblk):
                k = SUBLANE * j + s - off
                if 0 <= k < conv_w:
                    term = blocks[j - first] * wrow[k]
                    acc = term if acc is None else acc + term
            return acc

        def rolled(s, blocks):
            return pltpu.roll(phase_sum(s, blocks, 0), SUBLANE - s, 0)

        def load_blocks(first_row):
            return [zb_ref[pl.ds(first_row + SUBLANE * j, SUBLANE), cs] for j in range(nblk - 1)]

        carry0 = tuple(rolled(s, load_blocks(0)) for s in range(1, SUBLANE))

        def body(r, carry):
            base = r * SUBLANE
            blk = load_blocks(base + SUBLANE)
            y = phase_sum(0, blk, 1) + bias
            new = tuple(rolled(s, blk) for s in range(1, SUBLANE))
            for s in range(1, SUBLANE):
                y = y + jnp.where(rowi < SUBLANE - s, carry[s - 1], new[s - 1])
            y_ref[pl.ds(base, SUBLANE), cs] = y
            return new

        carry = carry0
        for r in range(tm // SUBLANE):
            carry = body(r, carry)


def _mixp_kernel(h_ref, sh_ref, sc_ref, gmix_ref, wq_ref, gql_ref, wuq_ref, gqh_ref,
                 wkv_ref, gkv_ref, wuk_ref, gkh_ref, wuvt_ref, wglu_ref, wgate_ref,
                 cq_ref, sq_ref, ck_ref, sk_ref, wdw_ref, bdw_ref, gcn_ref, bcn_ref,
                 kvlat_ref, krope_ref, qx_ref, kx_ref, vt_ref, ycv_ref, cst_ref, gl_ref,
                 zb_ref, y_ref, *, n_heads, d_head, kv_lora, rope, conv_w, conv_ch):
    t = pl.program_id(1)
    tm = h_ref.shape[0]

    @pl.when((pl.program_id(0) == 0) & (t == 0))
    def _():
        zb_ref[0:CONV_PAD, :] = jnp.zeros((CONV_PAD, conv_ch), F32)

    u = (_rms(h_ref[...], gmix_ref[...]) * (1.0 + sc_ref[...]) + sh_ref[...]).astype(BF16)

    glu = _dot(u, wglu_ref[...])
    z = glu[:, :conv_ch] * jax.nn.sigmoid(glu[:, conv_ch:])
    zb_ref[0:CONV_PAD, :] = jnp.where(t == 0, 0.0, zb_ref[0:CONV_PAD, :])
    zb_ref[CONV_PAD:CONV_PAD + tm, :] = z
    _causal_dwconv(zb_ref, wdw_ref, bdw_ref, y_ref, tm, conv_w, conv_ch)
    tail = zb_ref[tm:tm + CONV_PAD, :]
    zb_ref[0:CONV_PAD, :] = tail
    cst_ref[...] = tail

    gl_ref[...] = _dot(u, wgate_ref[...]).astype(gl_ref.dtype)

    qn = _rms(_dot(u, wq_ref[...]), gql_ref[...]).astype(BF16)
    _q_heads(qn, wuq_ref, cq_ref[...], sq_ref[...], gqh_ref[...], qx_ref, n_heads, d_head)

    kv_lat, kr = _latent_paths(u, wkv_ref, gkv_ref, ck_ref[...], sk_ref[...], kv_lora, rope)
    kvlat_ref[...] = kv_lat
    krope_ref[...] = kr[:, :rope]
    kv_bf = kv_lat.astype(BF16)
    _head_norm_store(_key_ext(kv_bf, kr, wuk_ref), gkh_ref[...], kx_ref, n_heads, d_head)
    vt_ref[...] = _dot_nt(wuvt_ref[...], kv_bf).astype(BF16)

    y = y_ref[...]
    mu = jnp.mean(y, axis=-1, keepdims=True)
    yc = y - mu
    var = jnp.mean(yc * yc, axis=-1, keepdims=True)
    yn = yc * lax.rsqrt(var + EPS) * gcn_ref[...] + bcn_ref[...]
    ycv_ref[...] = _silu(yn).astype(BF16)


def _mixer_prompt(h, shift, scale, wts, tabs, dims, tm):
    b, t, d = h.shape
    nh, dh, kvl, rope, cw_, cch = dims
    nt = t // tm
    hw = nh * SLAB
    kern = functools.partial(_mixp_kernel, n_heads=nh, d_head=dh, kv_lora=kvl, rope=rope,
                             conv_w=cw_, conv_ch=cch)
    tok = lambda n: pl.BlockSpec((None, tm, n), lambda i, j: (i, j, 0))
    mod = pl.BlockSpec((None, 1, d), lambda i, j: (i, 0, 0))
    tab = pl.BlockSpec((tm, LANE), lambda i, j: (j, 0))
    wnames = ("g_mix", "wq", "g_q_lat", "wuq", "gqh_exp2", "wkv", "g_kv_lat", "wuk", "gkh", "wuvt", "wglu",
              "wgate")
    cnames = ("wdw", "b_dw", "g_cn", "b_cn")
    vdim = wts["wuvt"].shape[0]
    gdim = wts["wgate"].shape[1]
    in_specs = ([tok(d), mod, mod] + [_const_spec(wts[n].shape) for n in wnames] + [tab] * 4
                + [_const_spec(wts[n].shape) for n in cnames])
    out_shape = (jax.ShapeDtypeStruct((b, t, kvl), F32), jax.ShapeDtypeStruct((b, t, rope), F32),
                 jax.ShapeDtypeStruct((b, t, hw), BF16), jax.ShapeDtypeStruct((b, t, hw), BF16),
                 jax.ShapeDtypeStruct((b, vdim, t), BF16), jax.ShapeDtypeStruct((b, t, cch), BF16),
                 jax.ShapeDtypeStruct((b, CONV_PAD, cch), F32), jax.ShapeDtypeStruct((b, t, gdim), BF16))
    out_specs = (tok(kvl), tok(rope), tok(hw), tok(hw),
                 pl.BlockSpec((None, vdim, tm), lambda i, j: (i, 0, j)), tok(cch),
                 pl.BlockSpec((None, CONV_PAD, cch), lambda i, j: (i, 0, 0)), tok(gdim))
    return pl.pallas_call(
        kern, grid=(b, nt), in_specs=in_specs, out_specs=out_specs, out_shape=out_shape,
        scratch_shapes=[pltpu.VMEM((tm + CONV_PAD, cch), F32), pltpu.VMEM((tm, cch), F32)],
        compiler_params=_cparams("arbitrary", "arbitrary"),
        name="mixer_prompt",
    )(h, shift, scale, *[wts[n] for n in wnames], *tabs, *[wts[n] for n in cnames])


def _attn_kernel(q_ref, k_ref, vt_ref, o_ref, *, tq, d_v):
    t = q_ref.shape[0]
    kv_i = lax.broadcasted_iota(jnp.int32, (tq, tq), 0)
    q_i = lax.broadcasted_iota(jnp.int32, (tq, tq), 1)
    causal = kv_i <= q_i
    ones = jnp.ones((2 * SUBLANE, t), BF16)
    for qi in range(t // tq):
        rows = slice(qi * tq, (qi + 1) * tq)
        n_kv = (qi + 1) * tq
        vt = jnp.concatenate([vt_ref[:, 0:n_kv], ones[:, 0:n_kv]], axis=0)
        halves = []
        for hh in range(2):
            lanes = slice(hh * SLAB, (hh + 1) * SLAB)
            q = q_ref[rows, lanes]
            sd = jnp.where(causal, _dot_nt(k_ref[rows, lanes], q), -jnp.inf)
            m = jnp.max(sd, axis=0, keepdims=True)
            if qi > 0:
                so = _dot_nt(k_ref[0:qi * tq, lanes], q)
                m = jnp.maximum(m, jnp.max(so, axis=0, keepdims=True))
                p = jnp.concatenate([jnp.exp2(so - m).astype(BF16), jnp.exp2(sd - m).astype(BF16)], axis=0)
            else:
                p = jnp.exp2(sd - m).astype(BF16)
            ot = _dot(vt, p)
            halves.append(ot[hh * d_v:(hh + 1) * d_v, :] / ot[2 * d_v:2 * d_v + 1, :])
        o_ref[rows, :] = jnp.concatenate(halves, axis=0).T.astype(o_ref.dtype)


def _attn_prompt(qx, kx, vt, n_heads, d_v, tq):
    b, t, _ = qx.shape
    pairs = n_heads // 2
    return pl.pallas_call(
        functools.partial(_attn_kernel, tq=tq, d_v=d_v),
        grid=(b, pairs),
        in_specs=[pl.BlockSpec((None, t, 2 * SLAB), lambda i, p: (i, 0, p)),
                  pl.BlockSpec((None, t, 2 * SLAB), lambda i, p: (i, 0, p)),
                  pl.BlockSpec((None, 2 * d_v, t), lambda i, p: (i, p, 0))],
        out_specs=pl.BlockSpec((None, t, 2 * d_v), lambda i, p: (i, 0, p)),
        out_shape=jax.ShapeDtypeStruct((b, t, n_heads * d_v), BF16),
        compiler_params=_cparams("arbitrary", "arbitrary"),
        name="attn_prompt",
    )(qx, kx, vt)


def _merge_kernel(h_ref, gate_ref, gl_ref, o_ref, y_ref, wao_ref, wco_ref, wout_ref, out_ref):
    d = wao_ref.shape[1]
    a = _dot(o_ref[...], wao_ref[...])
    b = _dot(y_ref[...], wco_ref[...])
    merged = (jax.nn.sigmoid(gl_ref[:, :d].astype(F32)) * a
              + jax.nn.sigmoid(gl_ref[:, d:].astype(F32)) * b)
    out_ref[...] = h_ref[...] + gate_ref[...] * _dot(merged.astype(BF16), wout_ref[...])


def _merge(h, gate, gl, o, ycv, wao, wco, wout, tm, tiles_per_seq):
    m, d = h.shape
    row = lambda n: pl.BlockSpec((tm, n), lambda i: (i, 0))
    return pl.pallas_call(
        _merge_kernel,
        grid=(m // tm,),
        in_specs=[row(d), _mod_spec(gate, tiles_per_seq), row(gl.shape[1]), row(o.shape[1]),
                  row(ycv.shape[1]), _const_spec(wao.shape), _const_spec(wco.shape),
                  _const_spec(wout.shape)],
        out_specs=row(d),
        out_shape=jax.ShapeDtypeStruct((m, d), F32),
        compiler_params=_cparams("arbitrary"),
        name="merge",
    )(h, gate, gl, o, ycv, wao, wco, wout)


def _mixs_kernel(h_ref, sh_ref, sc_ref, gmix_ref, wq_ref, gql_ref, wuq_ref, gqh_ref,
                 wkv_ref, gkv_ref, wuk_ref, gkh_ref, wukt_ref, wglu_ref, wgate_ref,
                 cq_ref, sq_ref, ck_ref, sk_ref, wdw_ref, bdw_ref, gcn_ref, bcn_ref, st_ref,
                 kvlat_ref, krope_ref, qx_ref, qg_ref, qt_ref, knew_ref, z_ref, ycv_ref, gl_ref,
                 *, n_heads, d_head, kv_lora, rope, conv_w, conv_ch):
    u = (_rms(h_ref[...], gmix_ref[...]) * (1.0 + sc_ref[...]) + sh_ref[...]).astype(BF16)
    gl_ref[...] = _dot(u, wgate_ref[...]).astype(gl_ref.dtype)

    qn = _rms(_dot(u, wq_ref[...]), gql_ref[...]).astype(BF16)
    _q_heads(qn, wuq_ref, cq_ref[0:1, :], sq_ref[0:1, :], gqh_ref[...], qx_ref, n_heads, d_head)
    for h in range(n_heads):
        qg = qx_ref[:, h * SLAB:(h + 1) * SLAB] * gkh_ref[...]
        qg_ref[:, h * SLAB:(h + 1) * SLAB] = qg
        qt_ref[:, h * kv_lora:(h + 1) * kv_lora] = _dot(qg.astype(BF16), wukt_ref[h])

    kv_lat, kr = _latent_paths(u, wkv_ref, gkv_ref, ck_ref[0:1, :], sk_ref[0:1, :], kv_lora, rope)
    kvlat_ref[...] = kv_lat
    krope_ref[...] = kr[:, :rope]
    _head_norm_store(_key_ext(kv_lat.astype(BF16), kr, wuk_ref), gkh_ref[...], knew_ref, n_heads, d_head)

    glu = _dot(u, wglu_ref[...])
    z = glu[:, :conv_ch] * jax.nn.sigmoid(glu[:, conv_ch:])
    z_ref[...] = z
    y = z * wdw_ref[conv_w - 1:conv_w, :] + bdw_ref[...]
    for k in range(conv_w - 1):
        y = y + st_ref[k] * wdw_ref[k:k + 1, :]
    mu = jnp.mean(y, axis=-1, keepdims=True)
    yc = y - mu
    var = jnp.mean(yc * yc, axis=-1, keepdims=True)
    yn = yc * lax.rsqrt(var + EPS) * gcn_ref[...] + bcn_ref[...]
    ycv_ref[...] = _silu(yn).astype(BF16)


def _mixer_sample(h, shift, scale, wts, tabs, state, dims, tm):
    m, d = h.shape
    nh, dh, kvl, rope, cw_, cch = dims
    hw = nh * SLAB
    kern = functools.partial(_mixs_kernel, n_heads=nh, d_head=dh, kv_lora=kvl, rope=rope,
                             conv_w=cw_, conv_ch=cch)
    row = lambda n: pl.BlockSpec((tm, n), lambda i: (i, 0))
    tab = pl.BlockSpec((8, LANE), lambda i: (0, 0))
    wnames = ("g_mix", "wq", "g_q_lat", "wuq", "gqh", "wkv", "g_kv_lat", "wuk", "gkh", "wukt", "wglu",
              "wgate")
    cnames = ("wdw", "b_dw", "g_cn", "b_cn")
    gdim = wts["wgate"].shape[1]
    in_specs = ([row(d), row(d), row(d)] + [_const_spec(wts[n].shape) for n in wnames] + [tab] * 4
                + [_const_spec(wts[n].shape) for n in cnames]
                + [pl.BlockSpec((cw_ - 1, tm, cch), lambda i: (0, i, 0))])
    out_shape = (jax.ShapeDtypeStruct((m, kvl), F32), jax.ShapeDtypeStruct((m, rope), F32),
                 jax.ShapeDtypeStruct((m, hw), F32), jax.ShapeDtypeStruct((m, hw), F32),
                 jax.ShapeDtypeStruct((m, nh * kvl), F32), jax.ShapeDtypeStruct((m, hw), F32),
                 jax.ShapeDtypeStruct((m, cch), F32), jax.ShapeDtypeStruct((m, cch), BF16),
                 jax.ShapeDtypeStruct((m, gdim), BF16))
    out_specs = (row(kvl), row(rope), row(hw), row(hw), row(nh * kvl), row(hw), row(cch), row(cch),
                 row(gdim))
    return pl.pallas_call(
        kern, grid=(m // tm,), in_specs=in_specs, out_specs=out_specs, out_shape=out_shape,
        compiler_params=_cparams("arbitrary"),
        name="mixer_sample",
    )(h, shift, scale, *[wts[n] for n in wnames], *tabs, *[wts[n] for n in cnames], state)


def _paged_kernel(pt_ref, qg_ref, qt_ref, qx_ref, knew_ref, cnew_ref, wukt_ref, kv_hbm, kr_hbm,
                  o_ref, kvbuf, krbuf, sem, cbf, s_scr, m_scr, l_scr, acc_scr,
                  *, pages, page, tile, n_heads, d_head, d_nope, rope, cps):
    b = pl.program_id(0)
    c2 = pl.program_id(1)
    nc = cps * pl.num_programs(1)
    nsteps = pl.num_programs(0) * nc
    npos = pages * page
    ppt = tile // page

    def copies(step_i, slot_i):
        out = []
        for p in range(pages):
            pg = pt_ref[step_i * pages + p]
            out.append(pltpu.make_async_copy(kv_hbm.at[pg], kvbuf.at[slot_i, p], sem.at[0, slot_i]))
            out.append(pltpu.make_async_copy(kr_hbm.at[pg], krbuf.at[slot_i, p], sem.at[1, slot_i]))
        return out

    def wait_slot(slot_i):
        pltpu.make_async_copy(kv_hbm.at[pl.ds(0, pages)], kvbuf.at[slot_i], sem.at[0, slot_i]).wait()
        pltpu.make_async_copy(kr_hbm.at[pl.ds(0, pages)], krbuf.at[slot_i], sem.at[1, slot_i]).wait()

    def softmax_update(m_prev, l_prev, acc_prev, s, cb):
        m_new = jnp.maximum(m_prev, jnp.max(s, axis=-1, keepdims=True))
        corr = jnp.exp(m_prev - m_new)
        p = jnp.exp(s - m_new)
        l_new = l_prev * corr + jnp.sum(p, axis=-1, keepdims=True)
        acc_new = acc_prev * corr + _dot(p.astype(BF16), cb)
        return m_new, l_new, acc_new

    @pl.when((b == 0) & (c2 == 0))
    def _():
        for cp in copies(0, 0):
            cp.start()
        s_scr[...] = jnp.zeros(s_scr.shape, F32)
        cbf[...] = jnp.zeros(cbf.shape, BF16)
        m_scr[...] = jnp.zeros(m_scr.shape, F32)
        l_scr[...] = jnp.zeros(l_scr.shape, F32)
        acc_scr[...] = jnp.zeros(acc_scr.shape, F32)

    qt_hi, qt_lo = _hilo(qt_ref[...])
    qtl = jnp.concatenate([qt_hi, qt_lo], axis=0)
    qg_hi, qg_lo = _hilo(qg_ref[:, d_nope:d_head])
    qgl = jnp.concatenate([qg_hi, qg_lo], axis=0)

    def chunk(i):
        slot = i % 2
        step = b * nc + cps * c2 + i
        wait_slot(slot)
        nxt = jnp.minimum(step + 1, nsteps - 1)
        for cp in copies(nxt, 1 - slot):
            cp.start()

        m_u, l_u, acc_u = softmax_update(m_scr[...], l_scr[...], acc_scr[...],
                                         s_scr[1 - slot], cbf[1 - slot])
        if i == 0:
            fresh = c2 == 0
            m_u = jnp.where(fresh, -jnp.inf, m_u)
            l_u = jnp.where(fresh, 0.0, l_u)
            acc_u = jnp.where(fresh, 0.0, acc_u)
        m_scr[...] = m_u
        l_scr[...] = l_u
        acc_scr[...] = acc_u

        krt = jnp.concatenate([krbuf[slot, p] for p in range(pages)], axis=1)
        krss = jnp.sum(krt * krt, axis=0, keepdims=True)
        sr = _dot(qgl, krt.astype(BF16))
        s_rope = sr[:n_heads] + sr[n_heads:]
        for j in range(npos // tile):
            cols = slice(j * tile, (j + 1) * tile)
            cb = kvbuf[slot, j * ppt:(j + 1) * ppt].reshape(tile, -1).astype(BF16)
            cbf[slot, cols, :] = cb
            kn = _dot_nt(wukt_ref[...], cb)
            ssn = jnp.sum((kn * kn).reshape(n_heads, d_nope, tile), axis=1)
            sn = _dot_nt(qtl, cb)
            rinv = lax.rsqrt((ssn + krss[:, cols]) * (1.0 / d_head) + EPS)
            s_scr[slot, :, cols] = (sn[:n_heads] + sn[n_heads:] + s_rope[:, cols]) * rinv

    for i in range(cps):
        chunk(i)
    last = (cps - 1) % 2

    @pl.when((b == pl.num_programs(0) - 1) & (c2 == pl.num_programs(1) - 1))
    def _():
        wait_slot(1 - last)

    @pl.when(c2 == pl.num_programs(1) - 1)
    def _():
        m_new, l_new, acc_new = softmax_update(m_scr[...], l_scr[...], acc_scr[...],
                                               s_scr[last], cbf[last])
        s_self = jnp.sum(qx_ref[...] * knew_ref[...], axis=-1, keepdims=True)
        m2 = jnp.maximum(m_new, s_self)
        corr2 = jnp.exp(m_new - m2)
        p_self = jnp.exp(s_self - m2)
        l2 = l_new * corr2 + p_self
        o_ref[...] = (acc_new * corr2 + p_self * cnew_ref[...]) / l2


def _paged_attn(page_table, qg, qt, qx, knew, cnew, wukt, cache_kv, cache_kr, dims, pages, tile):
    nb, n_pages = page_table.shape
    nh, dh, kvl, rope = dims
    d_nope = dh - rope
    page = cache_kv.shape[1]
    cps = n_pages // pages
    assert n_pages % pages == 0 and cps % 2 == 0 and tile % page == 0
    npos = pages * page
    kern = functools.partial(_paged_kernel, pages=pages, page=page, tile=tile, n_heads=nh,
                             d_head=dh, d_nope=d_nope, rope=rope, cps=cps)
    per_b = lambda r, n: pl.BlockSpec((None, r, n), lambda i, j, pt: (i, 0, 0))
    grid_spec = pltpu.PrefetchScalarGridSpec(
        num_scalar_prefetch=1,
        grid=(nb, 1),
        in_specs=[per_b(nh, SLAB), per_b(nh, kvl), per_b(nh, SLAB), per_b(nh, SLAB), per_b(1, kvl),
                  pl.BlockSpec(wukt.shape, lambda i, j, pt: (0, 0)),
                  pl.BlockSpec(memory_space=pl.ANY), pl.BlockSpec(memory_space=pl.ANY)],
        out_specs=per_b(nh, kvl),
        scratch_shapes=[pltpu.VMEM((2, pages, page, kvl), F32), pltpu.VMEM((2, pages, rope, page), F32),
                        pltpu.SemaphoreType.DMA((2, 2)),
                        pltpu.VMEM((2, npos, kvl), BF16), pltpu.VMEM((2, nh, npos), F32),
                        pltpu.VMEM((nh, 1), F32), pltpu.VMEM((nh, 1), F32), pltpu.VMEM((nh, kvl), F32)])
    return pl.pallas_call(
        kern, grid_spec=grid_spec,
        out_shape=jax.ShapeDtypeStruct((nb, nh, kvl), F32),
        compiler_params=_cparams("arbitrary", "arbitrary"),
        name="paged_attn",
    )(page_table.reshape(-1), qg, qt, qx, knew, cnew, wukt, cache_kv, cache_kr)


def _uvup_kernel(ol_ref, wa_ref, wb_ref, o_ref):
    for p in range(wa_ref.shape[0]):
        a = _dot(ol_ref[2 * p].astype(BF16), wa_ref[p])
        b = _dot(ol_ref[2 * p + 1].astype(BF16), wb_ref[p])
        o_ref[:, p * LANE:(p + 1) * LANE] = (a + b).astype(o_ref.dtype)


def _uv_up(olat_t, wa, wb):
    nh, m, _ = olat_t.shape
    return pl.pallas_call(
        _uvup_kernel,
        out_shape=jax.ShapeDtypeStruct((m, wa.shape[0] * LANE), BF16),
        compiler_params=pltpu.CompilerParams(vmem_limit_bytes=VMEM_LIMIT_BYTES),
        name="uv_up",
    )(olat_t, wa, wb)


def _prep_weights(w_in, g_norm_mix, g_q_lat, w_uq, g_q_head, g_kv_lat, w_uk, w_uv, g_k_head,
                  w_dw, b_dw, g_conv_norm, b_conv_norm):
    d = w_in.shape[0]
    q_lora, nh, dh = w_uq.shape
    kvl, _, d_nope = w_uk.shape
    rope = dh - d_nope
    half = rope // 2
    conv_w, cch = w_dw.shape
    pad = SLAB - dh
    c0, c1, c2, c3 = q_lora, q_lora + kvl, q_lora + kvl + rope, q_lora + kvl + rope + 2 * cch

    def rot_half(x):
        return jnp.concatenate([-x[..., half:], x[..., :half]], axis=-1)

    pe = w_in[:, c1:c2]
    zpad = jnp.zeros((d, LANE - rope), F32)
    wkv = jnp.concatenate([w_in[:, c0:c1], pe, zpad, rot_half(pe), zpad], axis=1)

    wuq_a = jnp.pad(w_uq, ((0, 0), (0, 0), (0, pad)))
    wuq_r = jnp.pad(rot_half(w_uq[..., d_nope:]), ((0, 0), (0, 0), (d_nope, pad)))
    wuq = jnp.concatenate([wuq_a.reshape(q_lora, nh * SLAB), wuq_r.reshape(q_lora, nh * SLAB)], axis=1)

    sel = jnp.zeros((rope, nh, SLAB), F32).at[:, :, d_nope:dh].set(
        jnp.broadcast_to(jnp.eye(rope, dtype=F32)[:, None, :], (rope, nh, rope)))
    sel = jnp.pad(sel.reshape(rope, nh * SLAB), ((0, LANE - rope), (0, 0)))
    wuk = jnp.concatenate([jnp.pad(w_uk, ((0, 0), (0, 0), (0, SLAB - d_nope))).reshape(kvl, nh * SLAB),
                           sel, sel], axis=0)
    wukt = jnp.pad(jnp.transpose(w_uk, (1, 2, 0)), ((0, 0), (0, SLAB - d_nope), (0, 0)))

    scale = dh ** -0.5
    return dict(
        g_mix=g_norm_mix.reshape(1, d), wq=w_in[:, :c0].astype(BF16), g_q_lat=g_q_lat.reshape(1, q_lora),
        wuq=wuq.astype(BF16), gqh=(jnp.pad(g_q_head, (0, pad)) * scale).reshape(1, SLAB),
        wkv=wkv.astype(BF16), g_kv_lat=g_kv_lat.reshape(1, kvl), wuk=wuk.astype(BF16),
        gqh_exp2=(jnp.pad(g_q_head, (0, pad)) * (scale * LOG2E)).reshape(1, SLAB),
        gkh=jnp.pad(g_k_head, (0, pad)).reshape(1, SLAB),
        wuvt=jnp.transpose(w_uv.reshape(kvl, -1)).astype(BF16),
        wukt=wukt.astype(BF16),
        wukt_flat=jnp.transpose(w_uk, (1, 2, 0)).reshape(nh * d_nope, kvl).astype(BF16),
        wglu=w_in[:, c2:c3].astype(BF16), wgate=w_in[:, c3:].astype(BF16),
        wdw=jnp.pad(w_dw, ((0, CONV_PAD - conv_w), (0, 0))), b_dw=b_dw.reshape(1, cch),
        g_cn=g_conv_norm.reshape(1, cch), b_cn=b_conv_norm.reshape(1, cch))


def kernel(x_prompt, x_sample, c_prompt, c_sample, cache_kv_latent, cache_k_rope, state_conv, page_table, w_ada, b_ada, g_norm_ffn1, w_ffn1_in, w_ffn1_out, g_norm_mix, w_in, g_q_lat, w_uq, g_q_head, g_kv_lat, w_uk, w_uv, g_k_head, w_attn_out, w_dw, b_dw, g_conv_norm, b_conv_norm, w_conv_out, w_out, g_norm_ffn2, w_ffn2_in, w_ffn2_out):
    bp, t, d = x_prompt.shape
    bs, ts, _ = x_sample.shape
    assert ts == 1, "sample group handles one new token per sequence"
    q_lora, nh, dh = w_uq.shape
    kvl, _, d_nope = w_uk.shape
    d_v = w_uv.shape[2]
    rope = dh - d_nope
    conv_w, cch = w_dw.shape
    n_pages = page_table.shape[1]
    page = cache_kv_latent.shape[1]
    past_len = n_pages * page
    dims = (nh, dh, kvl, rope, conv_w, cch)

    tm_ffn = min(512, t)
    tm_mix = min(256, t)
    tm_merge = min(512, t)
    tq = min(512, t)
    tm_s = 32
    pages_per_step = min(16, n_pages)
    pos_tile = 512

    wts = _prep_weights(w_in, g_norm_mix, g_q_lat, w_uq, g_q_head, g_kv_lat, w_uk, w_uv, g_k_head,
                        w_dw, b_dw, g_conv_norm, b_conv_norm)
    w1i, w1o = w_ffn1_in.astype(BF16), w_ffn1_out.astype(BF16)
    w2i, w2o = w_ffn2_in.astype(BF16), w_ffn2_out.astype(BF16)
    wao = w_attn_out.reshape(nh * d_v, d).astype(BF16)
    wco = w_conv_out.astype(BF16)
    wo = w_out.astype(BF16)
    g1, g2 = g_norm_ffn1.reshape(1, d), g_norm_ffn2.reshape(1, d)

    mod = _ada(jnp.concatenate([c_prompt, c_sample], axis=0), w_ada, b_ada)
    n_mod = mod.shape[1] // d
    mod_p = [mod[:bp, i * d:(i + 1) * d].reshape(bp, 1, d) for i in range(n_mod)]
    mod_s = [mod[bp:, i * d:(i + 1) * d] for i in range(n_mod)]
    mod_s3 = [m_.reshape(1, bs, d) for m_ in mod_s]

    half = rope // 2
    inv = ROPE_THETA ** (-jnp.arange(half, dtype=F32) / half)
    inv_q = jnp.zeros((LANE,), F32).at[d_nope:dh].set(jnp.tile(inv, 2))
    inv_k = jnp.zeros((LANE,), F32).at[:rope].set(jnp.tile(inv, 2))
    inv2 = jnp.stack([inv_q, inv_k])
    tabs_p = _rope_tables(inv2, t, 0)
    tabs_s = _rope_tables(inv2, 8, past_len)

    xp = x_prompt.reshape(bp * t, d)
    hp = _ffn(xp, mod_p[0], mod_p[1], mod_p[2], g1, w1i, w1o, tm_ffn, t // tm_ffn)
    kvlat_p, krope_p, qx, kx, vt, ycv_p, cst, gl_p = _mixer_prompt(
        hp.reshape(bp, t, d), mod_p[3], mod_p[4], wts, tabs_p, dims, tm_mix)
    o_p = _attn_prompt(qx, kx, vt, nh, d_v, tq)
    h2p = _merge(hp, mod_p[5], gl_p.reshape(bp * t, -1), o_p.reshape(bp * t, nh * d_v),
                 ycv_p.reshape(bp * t, cch), wao, wco, wo, tm_merge, t // tm_merge)
    y_p = _ffn(h2p, mod_p[6], mod_p[7], mod_p[8], g2, w2i, w2o, tm_ffn, t // tm_ffn)
    conv_state_p = cst[:, CONV_PAD - (conv_w - 1):, :]

    xs = x_sample.reshape(bs, d)
    hs = _ffn(xs, mod_s3[0], mod_s3[1], mod_s3[2], g1, w1i, w1o, bs, 1)
    kvlat_s, krope_s, qx_s, qg_s, qt_s, knew_s, z_s, ycv_s, gl_s = _mixer_sample(
        hs, mod_s[3], mod_s[4], wts, tabs_s, jnp.transpose(state_conv, (1, 0, 2)), dims, tm_s)
    olat = _paged_attn(page_table, qg_s.reshape(bs, nh, SLAB), qt_s.reshape(bs, nh, kvl),
                       qx_s.reshape(bs, nh, SLAB), knew_s.reshape(bs, nh, SLAB),
                       kvlat_s.reshape(bs, 1, kvl), wts["wukt_flat"], cache_kv_latent,
                       jnp.swapaxes(cache_k_rope, 1, 2),
                       (nh, dh, kvl, rope), pages_per_step, pos_tile)
    wuv_pair = w_uv.reshape(kvl, nh // 2, 2, d_v)
    zero = jnp.zeros((kvl, nh // 2, d_v), F32)
    wuv_a = jnp.transpose(jnp.concatenate([wuv_pair[:, :, 0], zero], axis=-1), (1, 0, 2)).astype(BF16)
    wuv_b = jnp.transpose(jnp.concatenate([zero, wuv_pair[:, :, 1]], axis=-1), (1, 0, 2)).astype(BF16)
    o_s = _uv_up(jnp.transpose(olat, (1, 0, 2)), wuv_a, wuv_b)
    h2s = _merge(hs, mod_s3[5], gl_s, o_s, ycv_s, wao, wco, wo, bs, 1)
    y_s = _ffn(h2s, mod_s3[6], mod_s3[7], mod_s3[8], g2, w2i, w2o, bs, 1)
    conv_state_s = jnp.concatenate([state_conv[:, 1:, :], z_s[:, None, :]], axis=1)

    return (y_p.reshape(bp, t, d), y_s.reshape(bs, 1, d), kvlat_p, krope_p, conv_state_p,
            kvlat_s.reshape(bs, 1, kvl), krope_s.reshape(bs, 1, rope), conv_state_s)
```

```python
import functools

import jax
import jax.numpy as jnp
from jax import lax
from jax.experimental import pallas as pl
from jax.experimental.pallas import tpu as pltpu

F32 = jnp.float32
BF16 = jnp.bfloat16
EPS = 1e-6
ROPE_THETA = 10000.0
LOG2E = 1.4426950408889634
LANE = 128
SUBLANE = 8
SLAB = 128
CONV_PAD = 32
VMEM_LIMIT_BYTES = 56 * 1024 * 1024


def _cparams(*sem):
    return pltpu.CompilerParams(dimension_semantics=sem, vmem_limit_bytes=VMEM_LIMIT_BYTES)


def _dot(a, b):
    return jnp.dot(a, b, preferred_element_type=F32)


def _dot_nt(a, b):
    return lax.dot_general(a, b, (((1,), (1,)), ((), ())), preferred_element_type=F32)


def _rms(x, g):
    return x * lax.rsqrt(jnp.mean(x * x, axis=-1, keepdims=True) + EPS) * g


def _silu(x):
    return x * jax.nn.sigmoid(x)


def _hilo(x):
    hi = x.astype(BF16)
    lo = (x - hi.astype(F32)).astype(BF16)
    return hi, lo


def _const_spec(shape):
    nd = len(shape)
    return pl.BlockSpec(shape, lambda *_: (0,) * nd, pipeline_mode=pl.Buffered(1))


def _ada_kernel(c_ref, w_ref, b_ref, o_ref):
    a = _silu(c_ref[...]).astype(BF16)
    o_ref[...] = _dot(a, w_ref[...].astype(BF16)) + b_ref[...]


def _ada(c_all, w_ada, b_ada):
    m, d = c_all.shape
    n = w_ada.shape[1]
    tn = n // 8
    return pl.pallas_call(
        _ada_kernel,
        grid=(n // tn,),
        in_specs=[pl.BlockSpec((m, d), lambda j: (0, 0)),
                  pl.BlockSpec((d, tn), lambda j: (0, j)),
                  pl.BlockSpec((1, tn), lambda j: (0, j))],
        out_specs=pl.BlockSpec((m, tn), lambda j: (0, j)),
        out_shape=jax.ShapeDtypeStruct((m, n), F32),
        compiler_params=_cparams("arbitrary"),
        name="ada",
    )(c_all, w_ada, b_ada.reshape(1, n))


def _rope_kernel(inv_ref, cq_ref, sq_ref, ck_ref, sk_ref, *, pos0):
    t = cq_ref.shape[0]
    pos = (lax.broadcasted_iota(jnp.int32, (t, LANE), 0) + pos0).astype(F32)
    aq = pos * inv_ref[0:1, :]
    ak = pos * inv_ref[1:2, :]
    cq_ref[...] = jnp.cos(aq)
    sq_ref[...] = jnp.sin(aq)
    ck_ref[...] = jnp.cos(ak)
    sk_ref[...] = jnp.sin(ak)


def _rope_tables(inv2, t, pos0):
    shp = jax.ShapeDtypeStruct((t, LANE), F32)
    return pl.pallas_call(
        functools.partial(_rope_kernel, pos0=pos0),
        out_shape=(shp, shp, shp, shp),
        name="rope_tables",
    )(inv2)


def _ffn_kernel(x_ref, sh_ref, sc_ref, gate_ref, g_ref, win_ref, wout_ref, o_ref, *, f, fc):
    x = x_ref[...]
    u = (_rms(x, g_ref[...]) * (1.0 + sc_ref[...]) + sh_ref[...]).astype(BF16)
    acc = None
    for c in range(f // fc):
        up = _dot(u, win_ref[:, c * fc:(c + 1) * fc])
        gt = _dot(u, win_ref[:, f + c * fc:f + (c + 1) * fc])
        a = (_silu(gt) * up).astype(BF16)
        part = _dot(a, wout_ref[c * fc:(c + 1) * fc, :])
        acc = part if acc is None else acc + part
    o_ref[...] = x + 0.5 * gate_ref[...] * acc


def _mod_spec(arr, tiles_per_seq):
    r, d = arr.shape[1:]
    return pl.BlockSpec((None, r, d), lambda i: (i // tiles_per_seq, 0, 0))


def _ffn(x, shift, scale, gate, g, w_in, w_out, tm, tiles_per_seq):
    m, d = x.shape
    f = w_out.shape[0]
    fc = f // 2
    return pl.pallas_call(
        functools.partial(_ffn_kernel, f=f, fc=fc),
        grid=(m // tm,),
        in_specs=[pl.BlockSpec((tm, d), lambda i: (i, 0)),
                  _mod_spec(shift, tiles_per_seq), _mod_spec(scale, tiles_per_seq),
                  _mod_spec(gate, tiles_per_seq),
                  _const_spec((1, d)), _const_spec(w_in.shape), _const_spec(w_out.shape)],
        out_specs=pl.BlockSpec((tm, d), lambda i: (i, 0)),
        out_shape=jax.ShapeDtypeStruct((m, d), F32),
        compiler_params=_cparams("arbitrary"),
        name="ffn",
    )(x, shift, scale, gate, g, w_in, w_out)


def _head_norm_store(x, gain, o_ref, n_heads, d_head):
    for h in range(n_heads):
        xh = x[:, h * SLAB:(h + 1) * SLAB]
        ss = jnp.sum(xh * xh, axis=-1, keepdims=True)
        rinv = lax.rsqrt(ss * (1.0 / d_head) + EPS)
        o_ref[:, h * SLAB:(h + 1) * SLAB] = (xh * rinv * gain).astype(o_ref.dtype)


def _q_heads(qn, wuq_ref, cq, sq, gain, o_ref, n_heads, d_head):
    hw = n_heads * SLAB
    group = 4
    for hc in range(n_heads // group):
        lo_, hi_ = hc * group * SLAB, (hc + 1) * group * SLAB
        a = _dot(qn, wuq_ref[:, lo_:hi_])
        r = _dot(qn, wuq_ref[:, hw + lo_:hw + hi_])
        for j in range(group):
            h = hc * group + j
            qh = a[:, j * SLAB:(j + 1) * SLAB] * cq + r[:, j * SLAB:(j + 1) * SLAB] * sq
            ss = jnp.sum(qh * qh, axis=-1, keepdims=True)
            rinv = lax.rsqrt(ss * (1.0 / d_head) + EPS)
            o_ref[:, h * SLAB:(h + 1) * SLAB] = (qh * rinv * gain).astype(o_ref.dtype)


def _latent_paths(u, wkv_ref, gkv_ref, ck, sk, kv_lora, rope):
    kvpe = _dot(u, wkv_ref[...])
    kv_lat = _rms(kvpe[:, :kv_lora], gkv_ref[...])
    kr = kvpe[:, kv_lora:kv_lora + LANE] * ck + kvpe[:, kv_lora + LANE:kv_lora + 2 * LANE] * sk
    return kv_lat, kr


def _key_ext(kv_lat_bf, kr, wuk_ref):
    kr_hi, kr_lo = _hilo(kr)
    x = jnp.concatenate([kv_lat_bf, kr_hi, kr_lo], axis=-1)
    return _dot(x, wuk_ref[...])


def _causal_dwconv(zb_ref, wdw_ref, bdw_ref, y_ref, tm, conv_w, conv_ch, cw=256):
    off = CONV_PAD - (conv_w - 1)
    nblk = (off + conv_w - 1) // SUBLANE + 1
    assert off > 0 and SUBLANE * (nblk - 1) + 1 - off >= conv_w
    rowi = lax.broadcasted_iota(jnp.int32, (SUBLANE, cw), 0)

    for cc in range(conv_ch // cw):
        cs = slice(cc * cw, (cc + 1) * cw)
        wrow = [wdw_ref[k:k + 1, cs] for k in range(conv_w)]
        bias = bdw_ref[:, cs]

        def phase_sum(s, blocks, first):
            acc = None
            for j in range(nblk):
                k = SUBLANE * j + s - off
                if 0 <= k < conv_w:
                    term = blocks[j - first] * wrow[k]
                    acc = term if acc is None else acc + term
            return acc

        def rolled(s, blocks):
            return pltpu.roll(phase_sum(s, blocks, 0), SUBLANE - s, 0)

        def load_blocks(first_row):
            return [zb_ref[pl.ds(first_row + SUBLANE * j, SUBLANE), cs] for j in range(nblk - 1)]

        carry0 = tuple(rolled(s, load_blocks(0)) for s in range(1, SUBLANE))

        def body(r, carry):
            base = r * SUBLANE
            blk = load_blocks(base + SUBLANE)
            y = phase_sum(0, blk, 1) + bias
            new = tuple(rolled(s, blk) for s in range(1, SUBLANE))
            for s in range(1, SUBLANE):
                y = y + jnp.where(rowi < SUBLANE - s, carry[s - 1], new[s - 1])
            y_ref[pl.ds(base, SUBLANE), cs] = y
            return new

        carry = carry0
        for r in range(tm // SUBLANE):
            carry = body(r, carry)


def _mixp_kernel(h_ref, sh_ref, sc_ref, gmix_ref, wq_ref, gql_ref, wuq_ref, gqh_ref,
                 wkv_ref, gkv_ref, wuk_ref, gkh_ref, wuvt_ref, wglu_ref, wgate_ref,
                 cq_ref, sq_ref, ck_ref, sk_ref, wdw_ref, bdw_ref, gcn_ref, bcn_ref,
                 kvlat_ref, krope_ref, qx_ref, kx_ref, vt_ref, ycv_ref, cst_ref, gl_ref,
                 zb_ref, y_ref, *, n_heads, d_head, kv_lora, rope, conv_w, conv_ch):
    t = pl.program_id(1)
    tm = h_ref.shape[0]

    @pl.when((pl.program_id(0) == 0) & (t == 0))
    def _():
        zb_ref[0:CONV_PAD, :] = jnp.zeros((CONV_PAD, conv_ch), F32)

    u = (_rms(h_ref[...], gmix_ref[...]) * (1.0 + sc_ref[...]) + sh_ref[...]).astype(BF16)

    glu = _dot(u, wglu_ref[...])
    z = glu[:, :conv_ch] * jax.nn.sigmoid(glu[:, conv_ch:])
    zb_ref[0:CONV_PAD, :] = jnp.where(t == 0, 0.0, zb_ref[0:CONV_PAD, :])
    zb_ref[CONV_PAD:CONV_PAD + tm, :] = z
    _causal_dwconv(zb_ref, wdw_ref, bdw_ref, y_ref, tm, conv_w, conv_ch)
    tail = zb_ref[tm:tm + CONV_PAD, :]
    zb_ref[0:CONV_PAD, :] = tail
    cst_ref[...] = tail

    gl_ref[...] = _dot(u, wgate_ref[...]).astype(gl_ref.dtype)

    qn = _rms(_dot(u, wq_ref[...]), gql_ref[...]).astype(BF16)
    _q_heads(qn, wuq_ref, cq_ref[...], sq_ref[...], gqh_ref[...], qx_ref, n_heads, d_head)

    kv_lat, kr = _latent_paths(u, wkv_ref, gkv_ref, ck_ref[...], sk_ref[...], kv_lora, rope)
    kvlat_ref[...] = kv_lat
    krope_ref[...] = kr[:, :rope]
    kv_bf = kv_lat.astype(BF16)
    _head_norm_store(_key_ext(kv_bf, kr, wuk_ref), gkh_ref[...], kx_ref, n_heads, d_head)
    vt_ref[...] = _dot_nt(wuvt_ref[...], kv_bf).astype(BF16)

    y = y_ref[...]
    mu = jnp.mean(y, axis=-1, keepdims=True)
    yc = y - mu
    var = jnp.mean(yc * yc, axis=-1, keepdims=True)
    yn = yc * lax.rsqrt(var + EPS) * gcn_ref[...] + bcn_ref[...]
    ycv_ref[...] = _silu(yn).astype(BF16)


def _mixer_prompt(h, shift, scale, wts, tabs, dims, tm):
    b, t, d = h.shape
    nh, dh, kvl, rope, cw_, cch = dims
    nt = t // tm
    hw = nh * SLAB
    kern = functools.partial(_mixp_kernel, n_heads=nh, d_head=dh, kv_lora=kvl, rope=rope,
                             conv_w=cw_, conv_ch=cch)
    tok = lambda n: pl.BlockSpec((None, tm, n), lambda i, j: (i, j, 0))
    mod = pl.BlockSpec((None, 1, d), lambda i, j: (i, 0, 0))
    tab = pl.BlockSpec((tm, LANE), lambda i, j: (j, 0))
    wnames = ("g_mix", "wq", "g_q_lat", "wuq", "gqh_exp2", "wkv", "g_kv_lat", "wuk", "gkh", "wuvt", "wglu",
              "wgate")
    cnames = ("wdw", "b_dw", "g_cn", "b_cn")
    vdim = wts["wuvt"].shape[0]
    gdim = wts["wgate"].shape[1]
    in_specs = ([tok(d), mod, mod] + [_const_spec(wts[n].shape) for n in wnames] + [tab] * 4
                + [_const_spec(wts[n].shape) for n in cnames])
    out_shape = (jax.ShapeDtypeStruct((b, t, kvl), F32), jax.ShapeDtypeStruct((b, t, rope), F32),
                 jax.ShapeDtypeStruct((b, t, hw), BF16), jax.ShapeDtypeStruct((b, t, hw), BF16),
                 jax.ShapeDtypeStruct((b, vdim, t), BF16), jax.ShapeDtypeStruct((b, t, cch), BF16),
                 jax.ShapeDtypeStruct((b, CONV_PAD, cch), F32), jax.ShapeDtypeStruct((b, t, gdim), BF16))
    out_specs = (tok(kvl), tok(rope), tok(hw), tok(hw),
                 pl.BlockSpec((None, vdim, tm), lambda i, j: (i, 0, j)), tok(cch),
                 pl.BlockSpec((None, CONV_PAD, cch), lambda i, j: (i, 0, 0)), tok(gdim))
    return pl.pallas_call(
        kern, grid=(b, nt), in_specs=in_specs, out_specs=out_specs, out_shape=out_shape,
        scratch_shapes=[pltpu.VMEM((tm + CONV_PAD, cch), F32), pltpu.VMEM((tm, cch), F32)],
        compiler_params=_cparams("arbitrary", "arbitrary"),
        name="mixer_prompt",
    )(h, shift, scale, *[wts[n] for n in wnames], *tabs, *[wts[n] for n in cnames])


def _attn_kernel(q_ref, k_ref, vt_ref, o_ref, *, tq, d_v):
    t = q_ref.shape[0]
    kv_i = lax.broadcasted_iota(jnp.int32, (tq, tq), 0)
    q_i = lax.broadcasted_iota(jnp.int32, (tq, tq), 1)
    causal = kv_i <= q_i
    ones = jnp.ones((2 * SUBLANE, t), BF16)
    for qi in range(t // tq):
        rows = slice(qi * tq, (qi + 1) * tq)
        n_kv = (qi + 1) * tq
        vt = jnp.concatenate([vt_ref[:, 0:n_kv], ones[:, 0:n_kv]], axis=0)
        halves = []
        for hh in range(2):
            lanes = slice(hh * SLAB, (hh + 1) * SLAB)
            q = q_ref[rows, lanes]
            sd = jnp.where(causal, _dot_nt(k_ref[rows, lanes], q), -jnp.inf)
            m = jnp.max(sd, axis=0, keepdims=True)
            if qi > 0:
                so = _dot_nt(k_ref[0:qi * tq, lanes], q)
                m = jnp.maximum(m, jnp.max(so, axis=0, keepdims=True))
                p = jnp.concatenate([jnp.exp2(so - m).astype(BF16), jnp.exp2(sd - m).astype(BF16)], axis=0)
            else:
                p = jnp.exp2(sd - m).astype(BF16)
            ot = _dot(vt, p)
            halves.append(ot[hh * d_v:(hh + 1) * d_v, :] / ot[2 * d_v:2 * d_v + 1, :])
        o_ref[rows, :] = jnp.concatenate(halves, axis=0).T.astype(o_ref.dtype)


def _attn_prompt(qx, kx, vt, n_heads, d_v, tq):
    b, t, _ = qx.shape
    pairs = n_heads // 2
    return pl.pallas_call(
        functools.partial(_attn_kernel, tq=tq, d_v=d_v),
        grid=(b, pairs),
        in_specs=[pl.BlockSpec((None, t, 2 * SLAB), lambda i, p: (i, 0, p)),
                  pl.BlockSpec((None, t, 2 * SLAB), lambda i, p: (i, 0, p)),
                  pl.BlockSpec((None, 2 * d_v, t), lambda i, p: (i, p, 0))],
        out_specs=pl.BlockSpec((None, t, 2 * d_v), lambda i, p: (i, 0, p)),
        out_shape=jax.ShapeDtypeStruct((b, t, n_heads * d_v), BF16),
        compiler_params=_cparams("arbitrary", "arbitrary"),
        name="attn_prompt",
    )(qx, kx, vt)


def _merge_kernel(h_ref, gate_ref, gl_ref, o_ref, y_ref, wao_ref, wco_ref, wout_ref, out_ref):
    d = wao_ref.shape[1]
    a = _dot(o_ref[...], wao_ref[...])
    b = _dot(y_ref[...], wco_ref[...])
    merged = (jax.nn.sigmoid(gl_ref[:, :d].astype(F32)) * a
              + jax.nn.sigmoid(gl_ref[:, d:].astype(F32)) * b)
    out_ref[...] = h_ref[...] + gate_ref[...] * _dot(merged.astype(BF16), wout_ref[...])


def _merge(h, gate, gl, o, ycv, wao, wco, wout, tm, tiles_per_seq):
    m, d = h.shape
    row = lambda n: pl.BlockSpec((tm, n), lambda i: (i, 0))
    return pl.pallas_call(
        _merge_kernel,
        grid=(m // tm,),
        in_specs=[row(d), _mod_spec(gate, tiles_per_seq), row(gl.shape[1]), row(o.shape[1]),
                  row(ycv.shape[1]), _const_spec(wao.shape), _const_spec(wco.shape),
                  _const_spec(wout.shape)],
        out_specs=row(d),
        out_shape=jax.ShapeDtypeStruct((m, d), F32),
        compiler_params=_cparams("arbitrary"),
        name="merge",
    )(h, gate, gl, o, ycv, wao, wco, wout)


def _mixs_kernel(h_ref, sh_ref, sc_ref, gmix_ref, wq_ref, gql_ref, wuq_ref, gqh_ref,
                 wkv_ref, gkv_ref, wuk_ref, gkh_ref, wukt_ref, wglu_ref, wgate_ref,
                 cq_ref, sq_ref, ck_ref, sk_ref, wdw_ref, bdw_ref, gcn_ref, bcn_ref, st_ref,
                 kvlat_ref, krope_ref, qx_ref, qg_ref, qt_ref, knew_ref, z_ref, ycv_ref, gl_ref,
                 *, n_heads, d_head, kv_lora, rope, conv_w, conv_ch):
    u = (_rms(h_ref[...], gmix_ref[...]) * (1.0 + sc_ref[...]) + sh_ref[...]).astype(BF16)
    gl_ref[...] = _dot(u, wgate_ref[...]).astype(gl_ref.dtype)

    qn = _rms(_dot(u, wq_ref[...]), gql_ref[...]).astype(BF16)
    _q_heads(qn, wuq_ref, cq_ref[0:1, :], sq_ref[0:1, :], gqh_ref[...], qx_ref, n_heads, d_head)
    for h in range(n_heads):
        qg = qx_ref[:, h * SLAB:(h + 1) * SLAB] * gkh_ref[...]
        qg_ref[:, h * SLAB:(h + 1) * SLAB] = qg
        qt_ref[:, h * kv_lora:(h + 1) * kv_lora] = _dot(qg.astype(BF16), wukt_ref[h])

    kv_lat, kr = _latent_paths(u, wkv_ref, gkv_ref, ck_ref[0:1, :], sk_ref[0:1, :], kv_lora, rope)
    kvlat_ref[...] = kv_lat
    krope_ref[...] = kr[:, :rope]
    _head_norm_store(_key_ext(kv_lat.astype(BF16), kr, wuk_ref), gkh_ref[...], knew_ref, n_heads, d_head)

    glu = _dot(u, wglu_ref[...])
    z = glu[:, :conv_ch] * jax.nn.sigmoid(glu[:, conv_ch:])
    z_ref[...] = z
    y = z * wdw_ref[conv_w - 1:conv_w, :] + bdw_ref[...]
    for k in range(conv_w - 1):
        y = y + st_ref[k] * wdw_ref[k:k + 1, :]
    mu = jnp.mean(y, axis=-1, keepdims=True)
    yc = y - mu
    var = jnp.mean(yc * yc, axis=-1, keepdims=True)
    yn = yc * lax.rsqrt(var + EPS) * gcn_ref[...] + bcn_ref[...]
    ycv_ref[...] = _silu(yn).astype(BF16)


def _mixer_sample(h, shift, scale, wts, tabs, state, dims, tm):
    m, d = h.shape
    nh, dh, kvl, rope, cw_, cch = dims
    hw = nh * SLAB
    kern = functools.partial(_mixs_kernel, n_heads=nh, d_head=dh, kv_lora=kvl, rope=rope,
                             conv_w=cw_, conv_ch=cch)
    row = lambda n: pl.BlockSpec((tm, n), lambda i: (i, 0))
    tab = pl.BlockSpec((8, LANE), lambda i: (0, 0))
    wnames = ("g_mix", "wq", "g_q_lat", "wuq", "gqh", "wkv", "g_kv_lat", "wuk", "gkh", "wukt", "wglu",
              "wgate")
    cnames = ("wdw", "b_dw", "g_cn", "b_cn")
    gdim = wts["wgate"].shape[1]
    in_specs = ([row(d), row(d), row(d)] + [_const_spec(wts[n].shape) for n in wnames] + [tab] * 4
                + [_const_spec(wts[n].shape) for n in cnames]
                + [pl.BlockSpec((cw_ - 1, tm, cch), lambda i: (0, i, 0))])
    out_shape = (jax.ShapeDtypeStruct((m, kvl), F32), jax.ShapeDtypeStruct((m, rope), F32),
                 jax.ShapeDtypeStruct((m, hw), F32), jax.ShapeDtypeStruct((m, hw), F32),
                 jax.ShapeDtypeStruct((m, nh * kvl), F32), jax.ShapeDtypeStruct((m, hw), F32),
                 jax.ShapeDtypeStruct((m, cch), F32), jax.ShapeDtypeStruct((m, cch), BF16),
                 jax.ShapeDtypeStruct((m, gdim), BF16))
    out_specs = (row(kvl), row(rope), row(hw), row(hw), row(nh * kvl), row(hw), row(cch), row(cch),
                 row(gdim))
    return pl.pallas_call(
        kern, grid=(m // tm,), in_specs=in_specs, out_specs=out_specs, out_shape=out_shape,
        compiler_params=_cparams("arbitrary"),
        name="mixer_sample",
    )(h, shift, scale, *[wts[n] for n in wnames], *tabs, *[wts[n] for n in cnames], state)


def _paged_kernel(pt_ref, qg_ref, qt_ref, qx_ref, knew_ref, cnew_ref, wukt_ref, kv_hbm, kr_hbm,
                  o_ref, kvbuf, krbuf, sem, cbf, s_scr, m_scr, l_scr, acc_scr,
                  *, pages, page, tile, n_heads, d_head, d_nope, rope, cps):
    b = pl.program_id(0)
    c2 = pl.program_id(1)
    nc = cps * pl.num_programs(1)
    nsteps = pl.num_programs(0) * nc
    npos = pages * page
    ppt = tile // page

    def copies(step_i, slot_i):
        out = []
        for p in range(pages):
            pg = pt_ref[step_i * pages + p]
            out.append(pltpu.make_async_copy(kv_hbm.at[pg], kvbuf.at[slot_i, p], sem.at[0, slot_i]))
            out.append(pltpu.make_async_copy(kr_hbm.at[pg], krbuf.at[slot_i, p], sem.at[1, slot_i]))
        return out

    def wait_slot(slot_i):
        pltpu.make_async_copy(kv_hbm.at[pl.ds(0, pages)], kvbuf.at[slot_i], sem.at[0, slot_i]).wait()
        pltpu.make_async_copy(kr_hbm.at[pl.ds(0, pages)], krbuf.at[slot_i], sem.at[1, slot_i]).wait()

    def softmax_update(m_prev, l_prev, acc_prev, s, cb):
        m_new = jnp.maximum(m_prev, jnp.max(s, axis=-1, keepdims=True))
        corr = jnp.exp(m_prev - m_new)
        p = jnp.exp(s - m_new)
        l_new = l_prev * corr + jnp.sum(p, axis=-1, keepdims=True)
        acc_new = acc_prev * corr + _dot(p.astype(BF16), cb)
        return m_new, l_new, acc_new

    @pl.when((b == 0) & (c2 == 0))
    def _():
        for cp in copies(0, 0):
            cp.start()
        s_scr[...] = jnp.zeros(s_scr.shape, F32)
        cbf[...] = jnp.zeros(cbf.shape, BF16)
        m_scr[...] = jnp.zeros(m_scr.shape, F32)
        l_scr[...] = jnp.zeros(l_scr.shape, F32)
        acc_scr[...] = jnp.zeros(acc_scr.shape, F32)

    qt_hi, qt_lo = _hilo(qt_ref[...])
    qtl = jnp.concatenate([qt_hi, qt_lo], axis=0)
    qg_hi, qg_lo = _hilo(qg_ref[:, d_nope:d_head])
    qgl = jnp.concatenate([qg_hi, qg_lo], axis=0)

    def chunk(i):
        slot = i % 2
        step = b * nc + cps * c2 + i
        wait_slot(slot)
        nxt = jnp.minimum(step + 1, nsteps - 1)
        for cp in copies(nxt, 1 - slot):
            cp.start()

        m_u, l_u, acc_u = softmax_update(m_scr[...], l_scr[...], acc_scr[...],
                                         s_scr[1 - slot], cbf[1 - slot])
        if i == 0:
            fresh = c2 == 0
            m_u = jnp.where(fresh, -jnp.inf, m_u)
            l_u = jnp.where(fresh, 0.0, l_u)
            acc_u = jnp.where(fresh, 0.0, acc_u)
        m_scr[...] = m_u
        l_scr[...] = l_u
        acc_scr[...] = acc_u

        krt = jnp.concatenate([krbuf[slot, p] for p in range(pages)], axis=1)
        krss = jnp.sum(krt * krt, axis=0, keepdims=True)
        sr = _dot(qgl, krt.astype(BF16))
        s_rope = sr[:n_heads] + sr[n_heads:]
        for j in range(npos // tile):
            cols = slice(j * tile, (j + 1) * tile)
            cb = kvbuf[slot, j * ppt:(j + 1) * ppt].reshape(tile, -1).astype(BF16)
            cbf[slot, cols, :] = cb
            hh = n_heads // 2
            ssn = []
            for half in range(2):
                kn = _dot_nt(wukt_ref[half * hh * d_nope:(half + 1) * hh * d_nope, :], cb)
                ssn.append(jnp.sum((kn * kn).reshape(hh, d_nope, tile), axis=1))
            ssn = jnp.concatenate(ssn, axis=0)
            sn = _dot_nt(qtl, cb)
            rinv = lax.rsqrt((ssn + krss[:, cols]) * (1.0 / d_head) + EPS)
            s_scr[slot, :, cols] = (sn[:n_heads] + sn[n_heads:] + s_rope[:, cols]) * rinv

    for i in range(cps):
        chunk(i)
    last = (cps - 1) % 2

    @pl.when((b == pl.num_programs(0) - 1) & (c2 == pl.num_programs(1) - 1))
    def _():
        wait_slot(1 - last)

    @pl.when(c2 == pl.num_programs(1) - 1)
    def _():
        m_new, l_new, acc_new = softmax_update(m_scr[...], l_scr[...], acc_scr[...],
                                               s_scr[last], cbf[last])
        s_self = jnp.sum(qx_ref[...] * knew_ref[...], axis=-1, keepdims=True)
        m2 = jnp.maximum(m_new, s_self)
        corr2 = jnp.exp(m_new - m2)
        p_self = jnp.exp(s_self - m2)
        l2 = l_new * corr2 + p_self
        o_ref[...] = (acc_new * corr2 + p_self * cnew_ref[...]) / l2


def _paged_attn(page_table, qg, qt, qx, knew, cnew, wukt, cache_kv, cache_kr, dims, pages, tile):
    nb, n_pages = page_table.shape
    nh, dh, kvl, rope = dims
    d_nope = dh - rope
    page = cache_kv.shape[1]
    n_chunks = n_pages // pages
    cps = min(4, n_chunks)
    assert n_pages % pages == 0 and n_chunks % cps == 0 and cps % 2 == 0 and tile % page == 0
    npos = pages * page
    kern = functools.partial(_paged_kernel, pages=pages, page=page, tile=tile, n_heads=nh,
                             d_head=dh, d_nope=d_nope, rope=rope, cps=cps)
    per_b = lambda r, n: pl.BlockSpec((None, r, n), lambda i, j, pt: (i, 0, 0))
    grid_spec = pltpu.PrefetchScalarGridSpec(
        num_scalar_prefetch=1,
        grid=(nb, n_chunks // cps),
        in_specs=[per_b(nh, SLAB), per_b(nh, kvl), per_b(nh, SLAB), per_b(nh, SLAB), per_b(1, kvl),
                  pl.BlockSpec(wukt.shape, lambda i, j, pt: (0, 0)),
                  pl.BlockSpec(memory_space=pl.ANY), pl.BlockSpec(memory_space=pl.ANY)],
        out_specs=per_b(nh, kvl),
        scratch_shapes=[pltpu.VMEM((2, pages, page, kvl), F32), pltpu.VMEM((2, pages, rope, page), F32),
                        pltpu.SemaphoreType.DMA((2, 2)),
                        pltpu.VMEM((2, npos, kvl), BF16), pltpu.VMEM((2, nh, npos), F32),
                        pltpu.VMEM((nh, 1), F32), pltpu.VMEM((nh, 1), F32), pltpu.VMEM((nh, kvl), F32)])
    return pl.pallas_call(
        kern, grid_spec=grid_spec,
        out_shape=jax.ShapeDtypeStruct((nb, nh, kvl), F32),
        compiler_params=_cparams("arbitrary", "arbitrary"),
        name="paged_attn",
    )(page_table.reshape(-1), qg, qt, qx, knew, cnew, wukt, cache_kv, cache_kr)


def _uvup_kernel(ol_ref, wa_ref, wb_ref, o_ref):
    for p in range(wa_ref.shape[0]):
        a = _dot(ol_ref[2 * p].astype(BF16), wa_ref[p])
        b = _dot(ol_ref[2 * p + 1].astype(BF16), wb_ref[p])
        o_ref[:, p * LANE:(p + 1) * LANE] = (a + b).astype(o_ref.dtype)


def _uv_up(olat_t, wa, wb):
    nh, m, _ = olat_t.shape
    return pl.pallas_call(
        _uvup_kernel,
        out_shape=jax.ShapeDtypeStruct((m, wa.shape[0] * LANE), BF16),
        compiler_params=pltpu.CompilerParams(vmem_limit_bytes=VMEM_LIMIT_BYTES),
        name="uv_up",
    )(olat_t, wa, wb)


def _prep_weights(w_in, g_norm_mix, g_q_lat, w_uq, g_q_head, g_kv_lat, w_uk, w_uv, g_k_head,
                  w_dw, b_dw, g_conv_norm, b_conv_norm):
    d = w_in.shape[0]
    q_lora, nh, dh = w_uq.shape
    kvl, _, d_nope = w_uk.shape
    rope = dh - d_nope
    half = rope // 2
    conv_w, cch = w_dw.shape
    pad = SLAB - dh
    c0, c1, c2, c3 = q_lora, q_lora + kvl, q_lora + kvl + rope, q_lora + kvl + rope + 2 * cch

    def rot_half(x):
        return jnp.concatenate([-x[..., half:], x[..., :half]], axis=-1)

    pe = w_in[:, c1:c2]
    zpad = jnp.zeros((d, LANE - rope), F32)
    wkv = jnp.concatenate([w_in[:, c0:c1], pe, zpad, rot_half(pe), zpad], axis=1)

    wuq_a = jnp.pad(w_uq, ((0, 0), (0, 0), (0, pad)))
    wuq_r = jnp.pad(rot_half(w_uq[..., d_nope:]), ((0, 0), (0, 0), (d_nope, pad)))
    wuq = jnp.concatenate([wuq_a.reshape(q_lora, nh * SLAB), wuq_r.reshape(q_lora, nh * SLAB)], axis=1)

    sel = jnp.zeros((rope, nh, SLAB), F32).at[:, :, d_nope:dh].set(
        jnp.broadcast_to(jnp.eye(rope, dtype=F32)[:, None, :], (rope, nh, rope)))
    sel = jnp.pad(sel.reshape(rope, nh * SLAB), ((0, LANE - rope), (0, 0)))
    wuk = jnp.concatenate([jnp.pad(w_uk, ((0, 0), (0, 0), (0, SLAB - d_nope))).reshape(kvl, nh * SLAB),
                           sel, sel], axis=0)
    wukt = jnp.pad(jnp.transpose(w_uk, (1, 2, 0)), ((0, 0), (0, SLAB - d_nope), (0, 0)))

    scale = dh ** -0.5
    return dict(
        g_mix=g_norm_mix.reshape(1, d), wq=w_in[:, :c0].astype(BF16), g_q_lat=g_q_lat.reshape(1, q_lora),
        wuq=wuq.astype(BF16), gqh=(jnp.pad(g_q_head, (0, pad)) * scale).reshape(1, SLAB),
        wkv=wkv.astype(BF16), g_kv_lat=g_kv_lat.reshape(1, kvl), wuk=wuk.astype(BF16),
        gqh_exp2=(jnp.pad(g_q_head, (0, pad)) * (scale * LOG2E)).reshape(1, SLAB),
        gkh=jnp.pad(g_k_head, (0, pad)).reshape(1, SLAB),
        wuvt=jnp.transpose(w_uv.reshape(kvl, -1)).astype(BF16),
        wukt=wukt.astype(BF16),
        wukt_flat=jnp.transpose(w_uk, (1, 2, 0)).reshape(nh * d_nope, kvl).astype(BF16),
        wglu=w_in[:, c2:c3].astype(BF16), wgate=w_in[:, c3:].astype(BF16),
        wdw=jnp.pad(w_dw, ((0, CONV_PAD - conv_w), (0, 0))), b_dw=b_dw.reshape(1, cch),
        g_cn=g_conv_norm.reshape(1, cch), b_cn=b_conv_norm.reshape(1, cch))


def kernel(x_prompt, x_sample, c_prompt, c_sample, cache_kv_latent, cache_k_rope, state_conv, page_table, w_ada, b_ada, g_norm_ffn1, w_ffn1_in, w_ffn1_out, g_norm_mix, w_in, g_q_lat, w_uq, g_q_head, g_kv_lat, w_uk, w_uv, g_k_head, w_attn_out, w_dw, b_dw, g_conv_norm, b_conv_norm, w_conv_out, w_out, g_norm_ffn2, w_ffn2_in, w_ffn2_out):
    bp, t, d = x_prompt.shape
    bs, ts, _ = x_sample.shape
    assert ts == 1, "sample group handles one new token per sequence"
    q_lora, nh, dh = w_uq.shape
    kvl, _, d_nope = w_uk.shape
    d_v = w_uv.shape[2]
    rope = dh - d_nope
    conv_w, cch = w_dw.shape
    n_pages = page_table.shape[1]
    page = cache_kv_latent.shape[1]
    past_len = n_pages * page
    dims = (nh, dh, kvl, rope, conv_w, cch)

    tm_ffn = min(512, t)
    tm_mix = min(256, t)
    tm_merge = min(512, t)
    tq = min(512, t)
    tm_s = 32
    pages_per_step = min(16, n_pages)
    pos_tile = 256

    wts = _prep_weights(w_in, g_norm_mix, g_q_lat, w_uq, g_q_head, g_kv_lat, w_uk, w_uv, g_k_head,
                        w_dw, b_dw, g_conv_norm, b_conv_norm)
    w1i, w1o = w_ffn1_in.astype(BF16), w_ffn1_out.astype(BF16)
    w2i, w2o = w_ffn2_in.astype(BF16), w_ffn2_out.astype(BF16)
    wao = w_attn_out.reshape(nh * d_v, d).astype(BF16)
    wco = w_conv_out.astype(BF16)
    wo = w_out.astype(BF16)
    g1, g2 = g_norm_ffn1.reshape(1, d), g_norm_ffn2.reshape(1, d)

    mod = _ada(jnp.concatenate([c_prompt, c_sample], axis=0), w_ada, b_ada)
    n_mod = mod.shape[1] // d
    mod_p = [mod[:bp, i * d:(i + 1) * d].reshape(bp, 1, d) for i in range(n_mod)]
    mod_s = [mod[bp:, i * d:(i + 1) * d] for i in range(n_mod)]
    mod_s3 = [m_.reshape(1, bs, d) for m_ in mod_s]

    half = rope // 2
    inv = ROPE_THETA ** (-jnp.arange(half, dtype=F32) / half)
    inv_q = jnp.zeros((LANE,), F32).at[d_nope:dh].set(jnp.tile(inv, 2))
    inv_k = jnp.zeros((LANE,), F32).at[:rope].set(jnp.tile(inv, 2))
    inv2 = jnp.stack([inv_q, inv_k])
    tabs_p = _rope_tables(inv2, t, 0)
    tabs_s = _rope_tables(inv2, 8, past_len)

    xp = x_prompt.reshape(bp * t, d)
    hp = _ffn(xp, mod_p[0], mod_p[1], mod_p[2], g1, w1i, w1o, tm_ffn, t // tm_ffn)
    kvlat_p, krope_p, qx, kx, vt, ycv_p, cst, gl_p = _mixer_prompt(
        hp.reshape(bp, t, d), mod_p[3], mod_p[4], wts, tabs_p, dims, tm_mix)
    o_p = _attn_prompt(qx, kx, vt, nh, d_v, tq)
    h2p = _merge(hp, mod_p[5], gl_p.reshape(bp * t, -1), o_p.reshape(bp * t, nh * d_v),
                 ycv_p.reshape(bp * t, cch), wao, wco, wo, tm_merge, t // tm_merge)
    y_p = _ffn(h2p, mod_p[6], mod_p[7], mod_p[8], g2, w2i, w2o, tm_ffn, t // tm_ffn)
    conv_state_p = cst[:, CONV_PAD - (conv_w - 1):, :]

    xs = x_sample.reshape(bs, d)
    hs = _ffn(xs, mod_s3[0], mod_s3[1], mod_s3[2], g1, w1i, w1o, bs, 1)
    kvlat_s, krope_s, qx_s, qg_s, qt_s, knew_s, z_s, ycv_s, gl_s = _mixer_sample(
        hs, mod_s[3], mod_s[4], wts, tabs_s, jnp.transpose(state_conv, (1, 0, 2)), dims, tm_s)
    olat = _paged_attn(page_table, qg_s.reshape(bs, nh, SLAB), qt_s.reshape(bs, nh, kvl),
                       qx_s.reshape(bs, nh, SLAB), knew_s.reshape(bs, nh, SLAB),
                       kvlat_s.reshape(bs, 1, kvl), wts["wukt_flat"], cache_kv_latent,
                       jnp.swapaxes(cache_k_rope, 1, 2),
                       (nh, dh, kvl, rope), pages_per_step, pos_tile)
    wuv_pair = w_uv.reshape(kvl, nh // 2, 2, d_v)
    zero = jnp.zeros((kvl, nh // 2, d_v), F32)
    wuv_a = jnp.transpose(jnp.concatenate([wuv_pair[:, :, 0], zero], axis=-1), (1, 0, 2)).astype(BF16)
    wuv_b = jnp.transpose(jnp.concatenate([zero, wuv_pair[:, :, 1]], axis=-1), (1, 0, 2)).astype(BF16)
    o_s = _uv_up(jnp.transpose(olat, (1, 0, 2)), wuv_a, wuv_b)
    h2s = _merge(hs, mod_s3[5], gl_s, o_s, ycv_s, wao, wco, wo, bs, 1)
    y_s = _ffn(h2s, mod_s3[6], mod_s3[7], mod_s3[8], g2, w2i, w2o, bs, 1)
    conv_state_s = jnp.concatenate([state_conv[:, 1:, :], z_s[:, None, :]], axis=1)

    return (y_p.reshape(bp, t, d), y_s.reshape(bs, 1, d), kvlat_p, krope_p, conv_state_p,
            kvlat_s.reshape(bs, 1, kvl), krope_s.reshape(bs, 1, rope), conv_state_s)
```

```python
import functools

import jax
import jax.numpy as jnp
from jax import lax
from jax.experimental import pallas as pl
from jax.experimental.pallas import tpu as pltpu

F32 = jnp.float32
BF16 = jnp.bfloat16
EPS = 1e-6
ROPE_THETA = 10000.0
LOG2E = 1.4426950408889634
LANE = 128
SUBLANE = 8
SLAB = 128
CONV_PAD = 32
VMEM_LIMIT_BYTES = 56 * 1024 * 1024


def _cparams(*sem):
    return pltpu.CompilerParams(dimension_semantics=sem, vmem_limit_bytes=VMEM_LIMIT_BYTES)


def _dot(a, b):
    return jnp.dot(a, b, preferred_element_type=F32)


def _dot_nt(a, b):
    return lax.dot_general(a, b, (((1,), (1,)), ((), ())), preferred_element_type=F32)


def _rms(x, g):
    return x * lax.rsqrt(jnp.mean(x * x, axis=-1, keepdims=True) + EPS) * g


def _silu(x):
    return x * jax.nn.sigmoid(x)


def _hilo(x):
    hi = x.astype(BF16)
    lo = (x - hi.astype(F32)).astype(BF16)
    return hi, lo


def _const_spec(shape):
    nd = len(shape)
    return pl.BlockSpec(shape, lambda *_: (0,) * nd, pipeline_mode=pl.Buffered(1))


def _ada_kernel(c_ref, w_ref, b_ref, o_ref):
    a = _silu(c_ref[...]).astype(BF16)
    o_ref[...] = _dot(a, w_ref[...].astype(BF16)) + b_ref[...]


def _ada(c_all, w_ada, b_ada):
    m, d = c_all.shape
    n = w_ada.shape[1]
    tn = n // 8
    return pl.pallas_call(
        _ada_kernel,
        grid=(n // tn,),
        in_specs=[pl.BlockSpec((m, d), lambda j: (0, 0)),
                  pl.BlockSpec((d, tn), lambda j: (0, j)),
                  pl.BlockSpec((1, tn), lambda j: (0, j))],
        out_specs=pl.BlockSpec((m, tn), lambda j: (0, j)),
        out_shape=jax.ShapeDtypeStruct((m, n), F32),
        compiler_params=_cparams("arbitrary"),
        name="ada",
    )(c_all, w_ada, b_ada.reshape(1, n))


def _rope_kernel(inv_ref, cq_ref, sq_ref, ck_ref, sk_ref, *, pos0):
    t = cq_ref.shape[0]
    pos = (lax.broadcasted_iota(jnp.int32, (t, LANE), 0) + pos0).astype(F32)
    aq = pos * inv_ref[0:1, :]
    ak = pos * inv_ref[1:2, :]
    cq_ref[...] = jnp.cos(aq)
    sq_ref[...] = jnp.sin(aq)
    ck_ref[...] = jnp.cos(ak)
    sk_ref[...] = jnp.sin(ak)


def _rope_tables(inv2, t, pos0):
    shp = jax.ShapeDtypeStruct((t, LANE), F32)
    return pl.pallas_call(
        functools.partial(_rope_kernel, pos0=pos0),
        out_shape=(shp, shp, shp, shp),
        name="rope_tables",
    )(inv2)


def _ffn_kernel(x_ref, sh_ref, sc_ref, gate_ref, g_ref, win_ref, wout_ref, o_ref, *, f, fc):
    x = x_ref[...]
    u = (_rms(x, g_ref[...]) * (1.0 + sc_ref[...]) + sh_ref[...]).astype(BF16)
    acc = None
    for c in range(f // fc):
        up = _dot(u, win_ref[:, c * fc:(c + 1) * fc])
        gt = _dot(u, win_ref[:, f + c * fc:f + (c + 1) * fc])
        a = (_silu(gt) * up).astype(BF16)
        part = _dot(a, wout_ref[c * fc:(c + 1) * fc, :])
        acc = part if acc is None else acc + part
    o_ref[...] = x + 0.5 * gate_ref[...] * acc


def _mod_spec(arr, tiles_per_seq):
    r, d = arr.shape[1:]
    return pl.BlockSpec((None, r, d), lambda i: (i // tiles_per_seq, 0, 0))


def _ffn(x, shift, scale, gate, g, w_in, w_out, tm, tiles_per_seq):
    m, d = x.shape
    f = w_out.shape[0]
    fc = f // 2
    return pl.pallas_call(
        functools.partial(_ffn_kernel, f=f, fc=fc),
        grid=(m // tm,),
        in_specs=[pl.BlockSpec((tm, d), lambda i: (i, 0)),
                  _mod_spec(shift, tiles_per_seq), _mod_spec(scale, tiles_per_seq),
                  _mod_spec(gate, tiles_per_seq),
                  _const_spec((1, d)), _const_spec(w_in.shape), _const_spec(w_out.shape)],
        out_specs=pl.BlockSpec((tm, d), lambda i: (i, 0)),
        out_shape=jax.ShapeDtypeStruct((m, d), F32),
        compiler_params=_cparams("arbitrary"),
        name="ffn",
    )(x, shift, scale, gate, g, w_in, w_out)


def _head_norm_store(x, gain, o_ref, n_heads, d_head):
    for h in range(n_heads):
        xh = x[:, h * SLAB:(h + 1) * SLAB]
        ss = jnp.sum(xh * xh, axis=-1, keepdims=True)
        rinv = lax.rsqrt(ss * (1.0 / d_head) + EPS)
        o_ref[:, h * SLAB:(h + 1) * SLAB] = (xh * rinv * gain).astype(o_ref.dtype)


def _q_heads(qn, wuq_ref, cq, sq, gain, o_ref, n_heads, d_head):
    hw = n_heads * SLAB
    group = 4
    for hc in range(n_heads // group):
        lo_, hi_ = hc * group * SLAB, (hc + 1) * group * SLAB
        a = _dot(qn, wuq_ref[:, lo_:hi_])
        r = _dot(qn, wuq_ref[:, hw + lo_:hw + hi_])
        for j in range(group):
            h = hc * group + j
            qh = a[:, j * SLAB:(j + 1) * SLAB] * cq + r[:, j * SLAB:(j + 1) * SLAB] * sq
            ss = jnp.sum(qh * qh, axis=-1, keepdims=True)
            rinv = lax.rsqrt(ss * (1.0 / d_head) + EPS)
            o_ref[:, h * SLAB:(h + 1) * SLAB] = (qh * rinv * gain).astype(o_ref.dtype)


def _latent_paths(u, wkv_ref, gkv_ref, ck, sk, kv_lora, rope):
    kvpe = _dot(u, wkv_ref[...])
    kv_lat = _rms(kvpe[:, :kv_lora], gkv_ref[...])
    kr = kvpe[:, kv_lora:kv_lora + LANE] * ck + kvpe[:, kv_lora + LANE:kv_lora + 2 * LANE] * sk
    return kv_lat, kr


def _key_ext(kv_lat_bf, kr, wuk_ref):
    kr_hi, kr_lo = _hilo(kr)
    x = jnp.concatenate([kv_lat_bf, kr_hi, kr_lo], axis=-1)
    return _dot(x, wuk_ref[...])


def _causal_dwconv(zb_ref, wdw_ref, bdw_ref, y_ref, tm, conv_w, conv_ch, cw=128):
    off = CONV_PAD - (conv_w - 1)
    nblk = (off + conv_w - 1) // SUBLANE + 1
    assert off > 0 and SUBLANE * (nblk - 1) + 1 - off >= conv_w
    rowi = lax.broadcasted_iota(jnp.int32, (SUBLANE, cw), 0)

    for cc in range(conv_ch // cw):
        cs = slice(cc * cw, (cc + 1) * cw)
        wrow = [wdw_ref[k:k + 1, cs] for k in range(conv_w)]
        bias = bdw_ref[:, cs]

        def phase_sum(s, blocks, first):
            acc = None
            for j in range(nblk):
                k = SUBLANE * j + s - off
                if 0 <= k < conv_w:
                    term = blocks[j - first] * wrow[k]
                    acc = term if acc is None else acc + term
            return acc

        def rolled(s, blocks):
            return pltpu.roll(phase_sum(s, blocks, 0), SUBLANE - s, 0)

        def load_blocks(first_row):
            return [zb_ref[pl.ds(first_row + SUBLANE * j, SUBLANE), cs] for j in range(nblk - 1)]

        carry0 = tuple(rolled(s, load_blocks(0)) for s in range(1, SUBLANE))

        def body(r, carry):
            base = r * SUBLANE
            blk = load_blocks(base + SUBLANE)
            y = phase_sum(0, blk, 1) + bias
            new = tuple(rolled(s, blk) for s in range(1, SUBLANE))
            for s in range(1, SUBLANE):
                y = y + jnp.where(rowi < SUBLANE - s, carry[s - 1], new[s - 1])
            y_ref[pl.ds(base, SUBLANE), cs] = y
            return new

        carry = carry0
        for r in range(tm // SUBLANE):
            carry = body(r, carry)


def _mixp_kernel(h_ref, sh_ref, sc_ref, gmix_ref, wq_ref, gql_ref, wuq_ref, gqh_ref,
                 wkv_ref, gkv_ref, wuk_ref, gkh_ref, wuvt_ref, wglu_ref, wgate_ref,
                 cq_ref, sq_ref, ck_ref, sk_ref, wdw_ref, bdw_ref, gcn_ref, bcn_ref,
                 kvlat_ref, krope_ref, qx_ref, kx_ref, vt_ref, ycv_ref, cst_ref, gl_ref,
                 zb_ref, y_ref, *, n_heads, d_head, kv_lora, rope, conv_w, conv_ch):
    t = pl.program_id(1)
    tm = h_ref.shape[0]

    @pl.when((pl.program_id(0) == 0) & (t == 0))
    def _():
        zb_ref[0:CONV_PAD, :] = jnp.zeros((CONV_PAD, conv_ch), F32)

    u = (_rms(h_ref[...], gmix_ref[...]) * (1.0 + sc_ref[...]) + sh_ref[...]).astype(BF16)

    glu = _dot(u, wglu_ref[...])
    z = glu[:, :conv_ch] * jax.nn.sigmoid(glu[:, conv_ch:])
    zb_ref[0:CONV_PAD, :] = jnp.where(t == 0, 0.0, zb_ref[0:CONV_PAD, :])
    zb_ref[CONV_PAD:CONV_PAD + tm, :] = z
    _causal_dwconv(zb_ref, wdw_ref, bdw_ref, y_ref, tm, conv_w, conv_ch)
    tail = zb_ref[tm:tm + CONV_PAD, :]
    zb_ref[0:CONV_PAD, :] = tail
    cst_ref[...] = tail

    gl_ref[...] = _dot(u, wgate_ref[...]).astype(gl_ref.dtype)

    qn = _rms(_dot(u, wq_ref[...]), gql_ref[...]).astype(BF16)
    _q_heads(qn, wuq_ref, cq_ref[...], sq_ref[...], gqh_ref[...], qx_ref, n_heads, d_head)

    kv_lat, kr = _latent_paths(u, wkv_ref, gkv_ref, ck_ref[...], sk_ref[...], kv_lora, rope)
    kvlat_ref[...] = kv_lat
    krope_ref[...] = kr[:, :rope]
    kv_bf = kv_lat.astype(BF16)
    _head_norm_store(_key_ext(kv_bf, kr, wuk_ref), gkh_ref[...], kx_ref, n_heads, d_head)
    vt_ref[...] = _dot_nt(wuvt_ref[...], kv_bf).astype(BF16)

    y = y_ref[...]
    mu = jnp.mean(y, axis=-1, keepdims=True)
    yc = y - mu
    var = jnp.mean(yc * yc, axis=-1, keepdims=True)
    yn = yc * lax.rsqrt(var + EPS) * gcn_ref[...] + bcn_ref[...]
    ycv_ref[...] = _silu(yn).astype(BF16)


def _mixer_prompt(h, shift, scale, wts, tabs, dims, tm):
    b, t, d = h.shape
    nh, dh, kvl, rope, cw_, cch = dims
    nt = t // tm
    hw = nh * SLAB
    kern = functools.partial(_mixp_kernel, n_heads=nh, d_head=dh, kv_lora=kvl, rope=rope,
                             conv_w=cw_, conv_ch=cch)
    tok = lambda n: pl.BlockSpec((None, tm, n), lambda i, j: (i, j, 0))
    mod = pl.BlockSpec((None, 1, d), lambda i, j: (i, 0, 0))
    tab = pl.BlockSpec((tm, LANE), lambda i, j: (j, 0))
    wnames = ("g_mix", "wq", "g_q_lat", "wuq", "gqh_exp2", "wkv", "g_kv_lat", "wuk", "gkh", "wuvt", "wglu",
              "wgate")
    cnames = ("wdw", "b_dw", "g_cn", "b_cn")
    vdim = wts["wuvt"].shape[0]
    gdim = wts["wgate"].shape[1]
    in_specs = ([tok(d), mod, mod] + [_const_spec(wts[n].shape) for n in wnames] + [tab] * 4
                + [_const_spec(wts[n].shape) for n in cnames])
    out_shape = (jax.ShapeDtypeStruct((b, t, kvl), F32), jax.ShapeDtypeStruct((b, t, rope), F32),
                 jax.ShapeDtypeStruct((b, t, hw), BF16), jax.ShapeDtypeStruct((b, t, hw), BF16),
                 jax.ShapeDtypeStruct((b, vdim, t), BF16), jax.ShapeDtypeStruct((b, t, cch), BF16),
                 jax.ShapeDtypeStruct((b, CONV_PAD, cch), F32), jax.ShapeDtypeStruct((b, t, gdim), BF16))
    out_specs = (tok(kvl), tok(rope), tok(hw), tok(hw),
                 pl.BlockSpec((None, vdim, tm), lambda i, j: (i, 0, j)), tok(cch),
                 pl.BlockSpec((None, CONV_PAD, cch), lambda i, j: (i, 0, 0)), tok(gdim))
    return pl.pallas_call(
        kern, grid=(b, nt), in_specs=in_specs, out_specs=out_specs, out_shape=out_shape,
        scratch_shapes=[pltpu.VMEM((tm + CONV_PAD, cch), F32), pltpu.VMEM((tm, cch), F32)],
        compiler_params=_cparams("arbitrary", "arbitrary"),
        name="mixer_prompt",
    )(h, shift, scale, *[wts[n] for n in wnames], *tabs, *[wts[n] for n in cnames])


def _attn_kernel(q_ref, k_ref, vt_ref, o_ref, *, tq, d_v):
    t = q_ref.shape[0]
    kv_i = lax.broadcasted_iota(jnp.int32, (tq, tq), 0)
    q_i = lax.broadcasted_iota(jnp.int32, (tq, tq), 1)
    causal = kv_i <= q_i
    ones = jnp.ones((2 * SUBLANE, t), BF16)
    for qi in range(t // tq):
        rows = slice(qi * tq, (qi + 1) * tq)
        n_kv = (qi + 1) * tq
        vt = jnp.concatenate([vt_ref[:, 0:n_kv], ones[:, 0:n_kv]], axis=0)
        halves = []
        for hh in range(2):
            lanes = slice(hh * SLAB, (hh + 1) * SLAB)
            q = q_ref[rows, lanes]
            sd = jnp.where(causal, _dot_nt(k_ref[rows, lanes], q), -jnp.inf)
            m = jnp.max(sd, axis=0, keepdims=True)
            if qi > 0:
                so = _dot_nt(k_ref[0:qi * tq, lanes], q)
                m = jnp.maximum(m, jnp.max(so, axis=0, keepdims=True))
                p = jnp.concatenate([jnp.exp2(so - m).astype(BF16), jnp.exp2(sd - m).astype(BF16)], axis=0)
            else:
                p = jnp.exp2(sd - m).astype(BF16)
            ot = _dot(vt, p)
            halves.append(ot[hh * d_v:(hh + 1) * d_v, :] / ot[2 * d_v:2 * d_v + 1, :])
        o_ref[rows, :] = jnp.concatenate(halves, axis=0).T.astype(o_ref.dtype)


def _attn_prompt(qx, kx, vt, n_heads, d_v, tq):
    b, t, _ = qx.shape
    pairs = n_heads // 2
    return pl.pallas_call(
        functools.partial(_attn_kernel, tq=tq, d_v=d_v),
        grid=(b, pairs),
        in_specs=[pl.BlockSpec((None, t, 2 * SLAB), lambda i, p: (i, 0, p)),
                  pl.BlockSpec((None, t, 2 * SLAB), lambda i, p: (i, 0, p)),
                  pl.BlockSpec((None, 2 * d_v, t), lambda i, p: (i, p, 0))],
        out_specs=pl.BlockSpec((None, t, 2 * d_v), lambda i, p: (i, 0, p)),
        out_shape=jax.ShapeDtypeStruct((b, t, n_heads * d_v), BF16),
        compiler_params=_cparams("arbitrary", "arbitrary"),
        name="attn_prompt",
    )(qx, kx, vt)


def _merge_kernel(h_ref, gate_ref, gl_ref, o_ref, y_ref, wao_ref, wco_ref, wout_ref, out_ref):
    d = wao_ref.shape[1]
    a = _dot(o_ref[...], wao_ref[...])
    b = _dot(y_ref[...], wco_ref[...])
    merged = (jax.nn.sigmoid(gl_ref[:, :d].astype(F32)) * a
              + jax.nn.sigmoid(gl_ref[:, d:].astype(F32)) * b)
    out_ref[...] = h_ref[...] + gate_ref[...] * _dot(merged.astype(BF16), wout_ref[...])


def _merge(h, gate, gl, o, ycv, wao, wco, wout, tm, tiles_per_seq):
    m, d = h.shape
    row = lambda n: pl.BlockSpec((tm, n), lambda i: (i, 0))
    return pl.pallas_call(
        _merge_kernel,
        grid=(m // tm,),
        in_specs=[row(d), _mod_spec(gate, tiles_per_seq), row(gl.shape[1]), row(o.shape[1]),
                  row(ycv.shape[1]), _const_spec(wao.shape), _const_spec(wco.shape),
                  _const_spec(wout.shape)],
        out_specs=row(d),
        out_shape=jax.ShapeDtypeStruct((m, d), F32),
        compiler_params=_cparams("arbitrary"),
        name="merge",
    )(h, gate, gl, o, ycv, wao, wco, wout)


def _mixs_kernel(h_ref, sh_ref, sc_ref, gmix_ref, wq_ref, gql_ref, wuq_ref, gqh_ref,
                 wkv_ref, gkv_ref, wuk_ref, gkh_ref, wukt_ref, wglu_ref, wgate_ref,
                 cq_ref, sq_ref, ck_ref, sk_ref, wdw_ref, bdw_ref, gcn_ref, bcn_ref, st_ref,
                 kvlat_ref, krope_ref, qx_ref, qg_ref, qt_ref, knew_ref, z_ref, ycv_ref, gl_ref,
                 *, n_heads, d_head, kv_lora, rope, conv_w, conv_ch):
    u = (_rms(h_ref[...], gmix_ref[...]) * (1.0 + sc_ref[...]) + sh_ref[...]).astype(BF16)
    gl_ref[...] = _dot(u, wgate_ref[...]).astype(gl_ref.dtype)

    qn = _rms(_dot(u, wq_ref[...]), gql_ref[...]).astype(BF16)
    _q_heads(qn, wuq_ref, cq_ref[0:1, :], sq_ref[0:1, :], gqh_ref[...], qx_ref, n_heads, d_head)
    for h in range(n_heads):
        qg = qx_ref[:, h * SLAB:(h + 1) * SLAB] * gkh_ref[...]
        qg_ref[:, h * SLAB:(h + 1) * SLAB] = qg
        qt_ref[:, h * kv_lora:(h + 1) * kv_lora] = _dot(qg.astype(BF16), wukt_ref[h])

    kv_lat, kr = _latent_paths(u, wkv_ref, gkv_ref, ck_ref[0:1, :], sk_ref[0:1, :], kv_lora, rope)
    kvlat_ref[...] = kv_lat
    krope_ref[...] = kr[:, :rope]
    _head_norm_store(_key_ext(kv_lat.astype(BF16), kr, wuk_ref), gkh_ref[...], knew_ref, n_heads, d_head)

    glu = _dot(u, wglu_ref[...])
    z = glu[:, :conv_ch] * jax.nn.sigmoid(glu[:, conv_ch:])
    z_ref[...] = z
    y = z * wdw_ref[conv_w - 1:conv_w, :] + bdw_ref[...]
    for k in range(conv_w - 1):
        y = y + st_ref[k] * wdw_ref[k:k + 1, :]
    mu = jnp.mean(y, axis=-1, keepdims=True)
    yc = y - mu
    var = jnp.mean(yc * yc, axis=-1, keepdims=True)
    yn = yc * lax.rsqrt(var + EPS) * gcn_ref[...] + bcn_ref[...]
    ycv_ref[...] = _silu(yn).astype(BF16)


def _mixer_sample(h, shift, scale, wts, tabs, state, dims, tm):
    m, d = h.shape
    nh, dh, kvl, rope, cw_, cch = dims
    hw = nh * SLAB
    kern = functools.partial(_mixs_kernel, n_heads=nh, d_head=dh, kv_lora=kvl, rope=rope,
                             conv_w=cw_, conv_ch=cch)
    row = lambda n: pl.BlockSpec((tm, n), lambda i: (i, 0))
    tab = pl.BlockSpec((8, LANE), lambda i: (0, 0))
    wnames = ("g_mix", "wq", "g_q_lat", "wuq", "gqh", "wkv", "g_kv_lat", "wuk", "gkh", "wukt", "wglu",
              "wgate")
    cnames = ("wdw", "b_dw", "g_cn", "b_cn")
    gdim = wts["wgate"].shape[1]
    in_specs = ([row(d), row(d), row(d)] + [_const_spec(wts[n].shape) for n in wnames] + [tab] * 4
                + [_const_spec(wts[n].shape) for n in cnames]
                + [pl.BlockSpec((cw_ - 1, tm, cch), lambda i: (0, i, 0))])
    out_shape = (jax.ShapeDtypeStruct((m, kvl), F32), jax.ShapeDtypeStruct((m, rope), F32),
                 jax.ShapeDtypeStruct((m, hw), F32), jax.ShapeDtypeStruct((m, hw), F32),
                 jax.ShapeDtypeStruct((m, nh * kvl), F32), jax.ShapeDtypeStruct((m, hw), F32),
                 jax.ShapeDtypeStruct((m, cch), F32), jax.ShapeDtypeStruct((m, cch), BF16),
                 jax.ShapeDtypeStruct((m, gdim), BF16))
    out_specs = (row(kvl), row(rope), row(hw), row(hw), row(nh * kvl), row(hw), row(cch), row(cch),
                 row(gdim))
    return pl.pallas_call(
        kern, grid=(m // tm,), in_specs=in_specs, out_specs=out_specs, out_shape=out_shape,
        compiler_params=_cparams("arbitrary"),
        name="mixer_sample",
    )(h, shift, scale, *[wts[n] for n in wnames], *tabs, *[wts[n] for n in cnames], state)


def _paged_kernel(pt_ref, qg_ref, qt_ref, qx_ref, knew_ref, cnew_ref, wukt_ref, kv_hbm, kr_hbm,
                  o_ref, kvbuf, krbuf, sem, cbf, s_scr, m_scr, l_scr, acc_scr,
                  *, pages, page, tile, n_heads, d_head, d_nope, rope, cps):
    b = pl.program_id(0)
    c2 = pl.program_id(1)
    nc = cps * pl.num_programs(1)
    nsteps = pl.num_programs(0) * nc
    npos = pages * page
    ppt = tile // page

    def copies(step_i, slot_i):
        out = []
        for p in range(pages):
            pg = pt_ref[step_i * pages + p]
            out.append(pltpu.make_async_copy(kv_hbm.at[pg], kvbuf.at[slot_i, p], sem.at[0, slot_i]))
            out.append(pltpu.make_async_copy(kr_hbm.at[pg], krbuf.at[slot_i, p], sem.at[1, slot_i]))
        return out

    def wait_slot(slot_i):
        pltpu.make_async_copy(kv_hbm.at[pl.ds(0, pages)], kvbuf.at[slot_i], sem.at[0, slot_i]).wait()
        pltpu.make_async_copy(kr_hbm.at[pl.ds(0, pages)], krbuf.at[slot_i], sem.at[1, slot_i]).wait()

    def softmax_update(m_prev, l_prev, acc_prev, s, cb):
        m_new = jnp.maximum(m_prev, jnp.max(s, axis=-1, keepdims=True))
        corr = jnp.exp(m_prev - m_new)
        p = jnp.exp(s - m_new)
        l_new = l_prev * corr + jnp.sum(p, axis=-1, keepdims=True)
        acc_new = acc_prev * corr + _dot(p.astype(BF16), cb)
        return m_new, l_new, acc_new

    @pl.when((b == 0) & (c2 == 0))
    def _():
        for cp in copies(0, 0):
            cp.start()
        s_scr[...] = jnp.zeros(s_scr.shape, F32)
        cbf[...] = jnp.zeros(cbf.shape, BF16)
        m_scr[...] = jnp.zeros(m_scr.shape, F32)
        l_scr[...] = jnp.zeros(l_scr.shape, F32)
        acc_scr[...] = jnp.zeros(acc_scr.shape, F32)

    qt_hi, qt_lo = _hilo(qt_ref[...])
    qtl = jnp.concatenate([qt_hi, qt_lo], axis=0)
    qg_hi, qg_lo = _hilo(qg_ref[:, d_nope:d_head])
    qgl = jnp.concatenate([qg_hi, qg_lo], axis=0)

    def chunk(i):
        slot = i % 2
        step = b * nc + cps * c2 + i
        wait_slot(slot)
        nxt = jnp.minimum(step + 1, nsteps - 1)
        for cp in copies(nxt, 1 - slot):
            cp.start()

        m_u, l_u, acc_u = softmax_update(m_scr[...], l_scr[...], acc_scr[...],
                                         s_scr[1 - slot], cbf[1 - slot])
        if i == 0:
            fresh = c2 == 0
            m_u = jnp.where(fresh, -jnp.inf, m_u)
            l_u = jnp.where(fresh, 0.0, l_u)
            acc_u = jnp.where(fresh, 0.0, acc_u)
        m_scr[...] = m_u
        l_scr[...] = l_u
        acc_scr[...] = acc_u

        krt = jnp.concatenate([krbuf[slot, p] for p in range(pages)], axis=1)
        krss = jnp.sum(krt * krt, axis=0, keepdims=True)
        sr = _dot(qgl, krt.astype(BF16))
        s_rope = sr[:n_heads] + sr[n_heads:]
        for j in range(npos // tile):
            cols = slice(j * tile, (j + 1) * tile)
            cb = kvbuf[slot, j * ppt:(j + 1) * ppt].reshape(tile, -1).astype(BF16)
            cbf[slot, cols, :] = cb
            hh = n_heads // 2
            ssn = []
            for half in range(2):
                kn = _dot_nt(wukt_ref[half * hh * d_nope:(half + 1) * hh * d_nope, :], cb)
                ssn.append(jnp.sum((kn * kn).reshape(hh, d_nope, tile), axis=1))
            ssn = jnp.concatenate(ssn, axis=0)
            sn = _dot_nt(qtl, cb)
            rinv = lax.rsqrt((ssn + krss[:, cols]) * (1.0 / d_head) + EPS)
            s_scr[slot, :, cols] = (sn[:n_heads] + sn[n_heads:] + s_rope[:, cols]) * rinv

    for i in range(cps):
        chunk(i)
    last = (cps - 1) % 2

    @pl.when((b == pl.num_programs(0) - 1) & (c2 == pl.num_programs(1) - 1))
    def _():
        wait_slot(1 - last)

    @pl.when(c2 == pl.num_programs(1) - 1)
    def _():
        m_new, l_new, acc_new = softmax_update(m_scr[...], l_scr[...], acc_scr[...],
                                               s_scr[last], cbf[last])
        s_self = jnp.sum(qx_ref[...] * knew_ref[...], axis=-1, keepdims=True)
        m2 = jnp.maximum(m_new, s_self)
        corr2 = jnp.exp(m_new - m2)
        p_self = jnp.exp(s_self - m2)
        l2 = l_new * corr2 + p_self
        o_ref[...] = (acc_new * corr2 + p_self * cnew_ref[...]) / l2


def _paged_attn(page_table, qg, qt, qx, knew, cnew, wukt, cache_kv, cache_kr, dims, pages, tile):
    nb, n_pages = page_table.shape
    nh, dh, kvl, rope = dims
    d_nope = dh - rope
    page = cache_kv.shape[1]
    n_chunks = n_pages // pages
    cps = min(max(2, 64 // pages), n_chunks)
    assert n_pages % pages == 0 and n_chunks % cps == 0 and cps % 2 == 0 and tile % page == 0
    npos = pages * page
    kern = functools.partial(_paged_kernel, pages=pages, page=page, tile=tile, n_heads=nh,
                             d_head=dh, d_nope=d_nope, rope=rope, cps=cps)
    per_b = lambda r, n: pl.BlockSpec((None, r, n), lambda i, j, pt: (i, 0, 0))
    grid_spec = pltpu.PrefetchScalarGridSpec(
        num_scalar_prefetch=1,
        grid=(nb, n_chunks // cps),
        in_specs=[per_b(nh, SLAB), per_b(nh, kvl), per_b(nh, SLAB), per_b(nh, SLAB), per_b(1, kvl),
                  pl.BlockSpec(wukt.shape, lambda i, j, pt: (0, 0)),
                  pl.BlockSpec(memory_space=pl.ANY), pl.BlockSpec(memory_space=pl.ANY)],
        out_specs=per_b(nh, kvl),
        scratch_shapes=[pltpu.VMEM((2, pages, page, kvl), F32), pltpu.VMEM((2, pages, rope, page), F32),
                        pltpu.SemaphoreType.DMA((2, 2)),
                        pltpu.VMEM((2, npos, kvl), BF16), pltpu.VMEM((2, nh, npos), F32),
                        pltpu.VMEM((nh, 1), F32), pltpu.VMEM((nh, 1), F32), pltpu.VMEM((nh, kvl), F32)])
    return pl.pallas_call(
        kern, grid_spec=grid_spec,
        out_shape=jax.ShapeDtypeStruct((nb, nh, kvl), F32),
        compiler_params=_cparams("arbitrary", "arbitrary"),
        name="paged_attn",
    )(page_table.reshape(-1), qg, qt, qx, knew, cnew, wukt, cache_kv, cache_kr)


def _uvup_kernel(ol_ref, wa_ref, wb_ref, o_ref):
    for p in range(wa_ref.shape[0]):
        a = _dot(ol_ref[2 * p].astype(BF16), wa_ref[p])
        b = _dot(ol_ref[2 * p + 1].astype(BF16), wb_ref[p])
        o_ref[:, p * LANE:(p + 1) * LANE] = (a + b).astype(o_ref.dtype)


def _uv_up(olat_t, wa, wb):
    nh, m, _ = olat_t.shape
    return pl.pallas_call(
        _uvup_kernel,
        out_shape=jax.ShapeDtypeStruct((m, wa.shape[0] * LANE), BF16),
        compiler_params=pltpu.CompilerParams(vmem_limit_bytes=VMEM_LIMIT_BYTES),
        name="uv_up",
    )(olat_t, wa, wb)


def _prep_weights(w_in, g_norm_mix, g_q_lat, w_uq, g_q_head, g_kv_lat, w_uk, w_uv, g_k_head,
                  w_dw, b_dw, g_conv_norm, b_conv_norm):
    d = w_in.shape[0]
    q_lora, nh, dh = w_uq.shape
    kvl, _, d_nope = w_uk.shape
    rope = dh - d_nope
    half = rope // 2
    conv_w, cch = w_dw.shape
    pad = SLAB - dh
    c0, c1, c2, c3 = q_lora, q_lora + kvl, q_lora + kvl + rope, q_lora + kvl + rope + 2 * cch

    def rot_half(x):
        return jnp.concatenate([-x[..., half:], x[..., :half]], axis=-1)

    pe = w_in[:, c1:c2]
    zpad = jnp.zeros((d, LANE - rope), F32)
    wkv = jnp.concatenate([w_in[:, c0:c1], pe, zpad, rot_half(pe), zpad], axis=1)

    wuq_a = jnp.pad(w_uq, ((0, 0), (0, 0), (0, pad)))
    wuq_r = jnp.pad(rot_half(w_uq[..., d_nope:]), ((0, 0), (0, 0), (d_nope, pad)))
    wuq = jnp.concatenate([wuq_a.reshape(q_lora, nh * SLAB), wuq_r.reshape(q_lora, nh * SLAB)], axis=1)

    sel = jnp.zeros((rope, nh, SLAB), F32).at[:, :, d_nope:dh].set(
        jnp.broadcast_to(jnp.eye(rope, dtype=F32)[:, None, :], (rope, nh, rope)))
    sel = jnp.pad(sel.reshape(rope, nh * SLAB), ((0, LANE - rope), (0, 0)))
    wuk = jnp.concatenate([jnp.pad(w_uk, ((0, 0), (0, 0), (0, SLAB - d_nope))).reshape(kvl, nh * SLAB),
                           sel, sel], axis=0)
    wukt = jnp.pad(jnp.transpose(w_uk, (1, 2, 0)), ((0, 0), (0, SLAB - d_nope), (0, 0)))

    scale = dh ** -0.5
    return dict(
        g_mix=g_norm_mix.reshape(1, d), wq=w_in[:, :c0].astype(BF16), g_q_lat=g_q_lat.reshape(1, q_lora),
        wuq=wuq.astype(BF16), gqh=(jnp.pad(g_q_head, (0, pad)) * scale).reshape(1, SLAB),
        wkv=wkv.astype(BF16), g_kv_lat=g_kv_lat.reshape(1, kvl), wuk=wuk.astype(BF16),
        gqh_exp2=(jnp.pad(g_q_head, (0, pad)) * (scale * LOG2E)).reshape(1, SLAB),
        gkh=jnp.pad(g_k_head, (0, pad)).reshape(1, SLAB),
        wuvt=jnp.transpose(w_uv.reshape(kvl, -1)).astype(BF16),
        wukt=wukt.astype(BF16),
        wukt_flat=jnp.transpose(w_uk, (1, 2, 0)).reshape(nh * d_nope, kvl).astype(BF16),
        wglu=w_in[:, c2:c3].astype(BF16), wgate=w_in[:, c3:].astype(BF16),
        wdw=jnp.pad(w_dw, ((0, CONV_PAD - conv_w), (0, 0))), b_dw=b_dw.reshape(1, cch),
        g_cn=g_conv_norm.reshape(1, cch), b_cn=b_conv_norm.reshape(1, cch))


def kernel(x_prompt, x_sample, c_prompt, c_sample, cache_kv_latent, cache_k_rope, state_conv, page_table, w_ada, b_ada, g_norm_ffn1, w_ffn1_in, w_ffn1_out, g_norm_mix, w_in, g_q_lat, w_uq, g_q_head, g_kv_lat, w_uk, w_uv, g_k_head, w_attn_out, w_dw, b_dw, g_conv_norm, b_conv_norm, w_conv_out, w_out, g_norm_ffn2, w_ffn2_in, w_ffn2_out):
    bp, t, d = x_prompt.shape
    bs, ts, _ = x_sample.shape
    assert ts == 1, "sample group handles one new token per sequence"
    q_lora, nh, dh = w_uq.shape
    kvl, _, d_nope = w_uk.shape
    d_v = w_uv.shape[2]
    rope = dh - d_nope
    conv_w, cch = w_dw.shape
    n_pages = page_table.shape[1]
    page = cache_kv_latent.shape[1]
    past_len = n_pages * page
    dims = (nh, dh, kvl, rope, conv_w, cch)

    tm_ffn = min(512, t)
    tm_mix = min(256, t)
    tm_merge = min(512, t)
    tq = min(512, t)
    tm_s = 32
    pages_per_step = min(32, n_pages)
    pos_tile = 256

    wts = _prep_weights(w_in, g_norm_mix, g_q_lat, w_uq, g_q_head, g_kv_lat, w_uk, w_uv, g_k_head,
                        w_dw, b_dw, g_conv_norm, b_conv_norm)
    w1i, w1o = w_ffn1_in.astype(BF16), w_ffn1_out.astype(BF16)
    w2i, w2o = w_ffn2_in.astype(BF16), w_ffn2_out.astype(BF16)
    wao = w_attn_out.reshape(nh * d_v, d).astype(BF16)
    wco = w_conv_out.astype(BF16)
    wo = w_out.astype(BF16)
    g1, g2 = g_norm_ffn1.reshape(1, d), g_norm_ffn2.reshape(1, d)

    mod = _ada(jnp.concatenate([c_prompt, c_sample], axis=0), w_ada, b_ada)
    n_mod = mod.shape[1] // d
    mod_p = [mod[:bp, i * d:(i + 1) * d].reshape(bp, 1, d) for i in range(n_mod)]
    mod_s = [mod[bp:, i * d:(i + 1) * d] for i in range(n_mod)]
    mod_s3 = [m_.reshape(1, bs, d) for m_ in mod_s]

    half = rope // 2
    inv = ROPE_THETA ** (-jnp.arange(half, dtype=F32) / half)
    inv_q = jnp.zeros((LANE,), F32).at[d_nope:dh].set(jnp.tile(inv, 2))
    inv_k = jnp.zeros((LANE,), F32).at[:rope].set(jnp.tile(inv, 2))
    inv2 = jnp.stack([inv_q, inv_k])
    tabs_p = _rope_tables(inv2, t, 0)
    tabs_s = _rope_tables(inv2, 8, past_len)

    xp = x_prompt.reshape(bp * t, d)
    hp = _ffn(xp, mod_p[0], mod_p[1], mod_p[2], g1, w1i, w1o, tm_ffn, t // tm_ffn)
    kvlat_p, krope_p, qx, kx, vt, ycv_p, cst, gl_p = _mixer_prompt(
        hp.reshape(bp, t, d), mod_p[3], mod_p[4], wts, tabs_p, dims, tm_mix)
    o_p = _attn_prompt(qx, kx, vt, nh, d_v, tq)
    h2p = _merge(hp, mod_p[5], gl_p.reshape(bp * t, -1), o_p.reshape(bp * t, nh * d_v),
                 ycv_p.reshape(bp * t, cch), wao, wco, wo, tm_merge, t // tm_merge)
    y_p = _ffn(h2p, mod_p[6], mod_p[7], mod_p[8], g2, w2i, w2o, tm_ffn, t // tm_ffn)
    conv_state_p = cst[:, CONV_PAD - (conv_w - 1):, :]

    xs = x_sample.reshape(bs, d)
    hs = _ffn(xs, mod_s3[0], mod_s3[1], mod_s3[2], g1, w1i, w1o, bs, 1)
    kvlat_s, krope_s, qx_s, qg_s, qt_s, knew_s, z_s, ycv_s, gl_s = _mixer_sample(
        hs, mod_s[3], mod_s[4], wts, tabs_s, jnp.transpose(state_conv, (1, 0, 2)), dims, tm_s)
    olat = _paged_attn(page_table, qg_s.reshape(bs, nh, SLAB), qt_s.reshape(bs, nh, kvl),
                       qx_s.reshape(bs, nh, SLAB), knew_s.reshape(bs, nh, SLAB),
                       kvlat_s.reshape(bs, 1, kvl), wts["wukt_flat"], cache_kv_latent,
                       jnp.swapaxes(cache_k_rope, 1, 2),
                       (nh, dh, kvl, rope), pages_per_step, pos_tile)
    wuv_pair = w_uv.reshape(kvl, nh // 2, 2, d_v)
    zero = jnp.zeros((kvl, nh // 2, d_v), F32)
    wuv_a = jnp.transpose(jnp.concatenate([wuv_pair[:, :, 0], zero], axis=-1), (1, 0, 2)).astype(BF16)
    wuv_b = jnp.transpose(jnp.concatenate([zero, wuv_pair[:, :, 1]], axis=-1), (1, 0, 2)).astype(BF16)
    o_s = _uv_up(jnp.transpose(olat, (1, 0, 2)), wuv_a, wuv_b)
    h2s = _merge(hs, mod_s3[5], gl_s, o_s, ycv_s, wao, wco, wo, bs, 1)
    y_s = _ffn(h2s, mod_s3[6], mod_s3[7], mod_s3[8], g2, w2i, w2o, bs, 1)
    conv_state_s = jnp.concatenate([state_conv[:, 1:, :], z_s[:, None, :]], axis=1)

    return (y_p.reshape(bp, t, d), y_s.reshape(bs, 1, d), kvlat_p, krope_p, conv_state_p,
            kvlat_s.reshape(bs, 1, kvl), krope_s.reshape(bs, 1, rope), conv_state_s)
```
